```python
import math
import jax, jax.numpy as jnp
from jax import lax
import numpy as np

D_MODEL = 1024
BATCH = 8
SEQ = 4096
DEPTH = 2
DEC_BATCH = 16
DEC_SEQ = 64
PAST_LEN = 1024

CHUNK = 64
N_MEM = 256
CONV_DIM = D_MODEL
CONV_WIDTH = 31
D_INNER = 2 * D_MODEL
SSD_HEAD_DIM = 64
SSD_HEADS = D_INNER // SSD_HEAD_DIM
SSD_GROUPS = 4
SSD_HPG = SSD_HEADS // SSD_GROUPS
SSD_STATE = 128
SSD_CONV_WIDTH = 4
SSD_XBC = D_INNER + 2 * SSD_GROUPS * SSD_STATE
XA_HEADS = 4
XA_HEAD_DIM = D_MODEL // XA_HEADS
XA_DIM = XA_HEADS * XA_HEAD_DIM
N_BRANCH = 3
IN_SIZES = (CONV_DIM, CONV_DIM, CONV_DIM, D_INNER, SSD_XBC, SSD_HEADS, XA_DIM, XA_DIM, N_BRANCH * D_MODEL)
N_IN = sum(IN_SIZES)
EPS = 1e-6

kernel_name = 'hybrid_conformer_ssd_memxattn_stream_step'


def _rmsnorm(x, w):
    x32 = x.astype(jnp.float32)
    y = x32 * lax.rsqrt(jnp.mean(x32 * x32, axis=-1, keepdims=True) + EPS)
    return (y * w.astype(jnp.float32)).astype(x.dtype)


def _layernorm(x, w, b):
    x32 = x.astype(jnp.float32)
    mu = jnp.mean(x32, axis=-1, keepdims=True)
    xc = x32 - mu
    y = xc * lax.rsqrt(jnp.mean(xc * xc, axis=-1, keepdims=True) + EPS)
    return (y * w.astype(jnp.float32) + b.astype(jnp.float32)).astype(x.dtype)


def _causal_dwconv(x, state, w, b):
    xp = jnp.concatenate([state.astype(x.dtype), x], axis=1)
    y = lax.conv_general_dilated(xp, w[:, None, :].astype(x.dtype), (1,), 'VALID',
                                 dimension_numbers=('NWC', 'WIO', 'NWC'),
                                 feature_group_count=x.shape[-1])
    return y + b.astype(x.dtype), xp[:, -(w.shape[0] - 1):]


def _ssd_scan(x, dt, a, bm, cm, h0):
    bsz, t = x.shape[0], x.shape[1]
    pad = (-t) % CHUNK
    if pad:
        x = jnp.pad(x, ((0, 0), (0, pad), (0, 0), (0, 0)))
        dt = jnp.pad(dt, ((0, 0), (0, pad), (0, 0)))
        bm = jnp.pad(bm, ((0, 0), (0, pad), (0, 0), (0, 0)))
        cm = jnp.pad(cm, ((0, 0), (0, pad), (0, 0), (0, 0)))
    nc = (t + pad) // CHUNK
    G, E, P, N = SSD_GROUPS, SSD_HPG, SSD_HEAD_DIM, SSD_STATE
    xc = x.reshape(bsz, nc, CHUNK, G, E, P)
    dtc = dt.reshape(bsz, nc, CHUNK, G, E)
    bc = bm.reshape(bsz, nc, CHUNK, G, N)
    cc = cm.reshape(bsz, nc, CHUNK, G, N)
    xdt = xc * dtc[..., None]
    acs = jnp.moveaxis(jnp.cumsum(dtc * a.reshape(G, E), axis=2), 2, -1)
    idx = jnp.arange(CHUNK)
    causal = idx[:, None] >= idx[None, :]
    lmat = jnp.exp(jnp.where(causal, acs[..., :, None] - acs[..., None, :], -jnp.inf))
    cb = jnp.einsum('bclgn,bcsgn->bcgls', cc, bc)
    y_diag = jnp.einsum('bcgels,bcsgep->bclgep', cb[:, :, :, None] * lmat, xdt)
    decay_to_end = jnp.exp(acs[..., -1:] - acs)
    decay_from_start = jnp.exp(acs)
    chunk_decay = jnp.exp(acs[..., -1])

    def step(h, inp):
        b_c, c_c, xdt_c, dte_c, dfs_c, cd_c = inp
        y_off = jnp.einsum('blgn,bgepn,bgel->blgep', c_c, h, dfs_c)
        h = h * cd_c[..., None, None] + jnp.einsum('blgn,bgel,blgep->bgepn', b_c, dte_c, xdt_c)
        return h, y_off

    xs = (jnp.moveaxis(bc, 1, 0), jnp.moveaxis(cc, 1, 0), jnp.moveaxis(xdt, 1, 0),
          jnp.moveaxis(decay_to_end, 1, 0), jnp.moveaxis(decay_from_start, 1, 0),
          jnp.moveaxis(chunk_decay, 1, 0))
    h_fin, y_off = lax.scan(step, h0.reshape(bsz, G, E, P, N), xs)
    y = y_diag + jnp.moveaxis(y_off, 0, 1)
    y = y.reshape(bsz, nc * CHUNK, SSD_HEADS, P)[:, :t]
    return y, h_fin.reshape(bsz, SSD_HEADS, P, N)


def _mem_kv(mem, mem_norm_w, xa_kv_w):
    kv = _rmsnorm(mem, mem_norm_w) @ xa_kv_w
    k, v = jnp.split(kv, 2, axis=-1)
    b = mem.shape[0]
    return (k.reshape(b, N_MEM, XA_HEADS, XA_HEAD_DIM), v.reshape(b, N_MEM, XA_HEADS, XA_HEAD_DIM))


def _layer(x, mem_k, mem_v, st_a, st_ssd, st_h,
           norm_pre_w, w_in, gate_b, conv_dw_w, conv_dw_b, conv_ln_w, conv_ln_b, conv_out_w,
           ssd_conv_w, ssd_conv_b, ssd_dt_bias, ssd_a_log, ssd_d, ssd_norm_w, ssd_out_w,
           xa_out_w, w_out, norm_post_w):
    f32 = jnp.float32
    bsz, t = x.shape[0], x.shape[1]
    h = _rmsnorm(x, norm_pre_w)
    proj = h @ w_in
    points = []
    acc = 0
    for s in IN_SIZES[:-1]:
        acc += s
        points.append(acc)
    glu_v, glu_g, conv_gate, z, xbc, dt_raw, q, xa_gate, gates = jnp.split(proj, points, axis=-1)

    u = glu_v * jax.nn.sigmoid(glu_g)
    c, new_a = _causal_dwconv(u, st_a, conv_dw_w, conv_dw_b)
    c = jax.nn.silu(_layernorm(c, conv_ln_w, conv_ln_b)) * jax.nn.silu(conv_gate)
    out_a = c @ conv_out_w

    xbc_c, new_ssd = _causal_dwconv(xbc, st_ssd, ssd_conv_w, ssd_conv_b)
    xbc_c = jax.nn.silu(xbc_c)
    xs, bm, cm = jnp.split(xbc_c, [D_INNER, D_INNER + SSD_GROUPS * SSD_STATE], axis=-1)
    dt = jax.nn.softplus(dt_raw.astype(f32) + ssd_dt_bias.astype(f32))
    a = -jnp.exp(ssd_a_log.astype(f32))
    xh = xs.astype(f32).reshape(bsz, t, SSD_HEADS, SSD_HEAD_DIM)
    y, new_h = _ssd_scan(xh, dt, a,
                         bm.astype(f32).reshape(bsz, t, SSD_GROUPS, SSD_STATE),
                         cm.astype(f32).reshape(bsz, t, SSD_GROUPS, SSD_STATE),
                         st_h.astype(f32))
    y = y + xh * ssd_d.astype(f32)[:, None]
    y = y.reshape(bsz, t, D_INNER) * jax.nn.silu(z.astype(f32))
    yg = y.reshape(bsz, t, SSD_GROUPS, D_INNER // SSD_GROUPS)
    yg = yg * lax.rsqrt(jnp.mean(yg * yg, axis=-1, keepdims=True) + EPS)
    y = yg.reshape(bsz, t, D_INNER) * ssd_norm_w.astype(f32)
    out_b = y.astype(x.dtype) @ ssd_out_w

    qh = q.reshape(bsz, t, XA_HEADS, XA_HEAD_DIM).astype(f32)
    s = jnp.einsum('bthd,bmhd->bhtm', qh, mem_k.astype(f32)) * (XA_HEAD_DIM ** -0.5)
    p = jax.nn.softmax(s, axis=-1)
    o = jnp.einsum('bhtm,bmhd->bthd', p, mem_v.astype(f32)).reshape(bsz, t, XA_DIM).astype(x.dtype)
    out_c = (o * jax.nn.silu(xa_gate)) @ xa_out_w

    g = jax.nn.sigmoid(gates + gate_b).reshape(bsz, t, N_BRANCH, D_MODEL)
    m = g[:, :, 0] * out_a + g[:, :, 1] * out_b + g[:, :, 2] * out_c
    x_new = x + _rmsnorm(m @ w_out, norm_post_w)
    return x_new, new_a, new_ssd, new_h.astype(st_h.dtype)


def setup_inputs(seed: int = 0) -> dict:
    key = jax.random.key(seed)
    ks = iter(jax.random.split(key, 40))

    def nrm(shape, scale):
        return jax.random.normal(next(ks), shape, jnp.float32) * scale

    L = DEPTH
    x_prompt = nrm((BATCH, SEQ, D_MODEL), 1.0)
    x_sample = nrm((DEC_BATCH, DEC_SEQ, D_MODEL), 1.0)
    mem_prompt = nrm((BATCH, N_MEM, D_MODEL), 1.0)
    cache_mem_k = nrm((L, DEC_BATCH, N_MEM, XA_HEADS, XA_HEAD_DIM), 1.0)
    cache_mem_v = nrm((L, DEC_BATCH, N_MEM, XA_HEADS, XA_HEAD_DIM), 1.0)
    state_conv_a = nrm((L, DEC_BATCH, CONV_WIDTH - 1, CONV_DIM), 0.5)
    state_conv_ssd = nrm((L, DEC_BATCH, SSD_CONV_WIDTH - 1, SSD_XBC), 1.0)
    state_ssm = nrm((L, DEC_BATCH, SSD_HEADS, SSD_HEAD_DIM, SSD_STATE), 0.1)

    norm_pre_w = 1.0 + nrm((L, D_MODEL), 0.02)
    w_in = nrm((L, D_MODEL, N_IN), D_MODEL ** -0.5)
    gate_b = nrm((L, N_BRANCH * D_MODEL), 0.02)
    conv_dw_w = nrm((L, CONV_WIDTH, CONV_DIM), CONV_WIDTH ** -0.5)
    conv_dw_b = nrm((L, CONV_DIM), 0.02)
    conv_ln_w = 1.0 + nrm((L, CONV_DIM), 0.02)
    conv_ln_b = nrm((L, CONV_DIM), 0.02)
    conv_out_w = nrm((L, CONV_DIM, D_MODEL), CONV_DIM ** -0.5)
    ssd_conv_w = nrm((L, SSD_CONV_WIDTH, SSD_XBC), SSD_CONV_WIDTH ** -0.5)
    ssd_conv_b = nrm((L, SSD_XBC), 0.02)
    u = jax.random.uniform(next(ks), (L, SSD_HEADS), jnp.float32)
    dt0 = jnp.exp(u * (math.log(0.1) - math.log(0.001)) + math.log(0.001))
    ssd_dt_bias = dt0 + jnp.log(-jnp.expm1(-dt0))
    ssd_a_log = jnp.log(jax.random.uniform(next(ks), (L, SSD_HEADS), jnp.float32, 1.0, 16.0))
    ssd_d = 1.0 + nrm((L, SSD_HEADS), 0.1)
    ssd_norm_w = 1.0 + nrm((L, D_INNER), 0.02)
    ssd_out_w = nrm((L, D_INNER, D_MODEL), D_INNER ** -0.5)
    mem_norm_w = 1.0 + nrm((L, D_MODEL), 0.02)
    xa_kv_w = nrm((L, D_MODEL, 2 * XA_DIM), D_MODEL ** -0.5)
    xa_out_w = nrm((L, XA_DIM, D_MODEL), XA_DIM ** -0.5)
    w_out = nrm((L, D_MODEL, D_MODEL), D_MODEL ** -0.5)
    norm_post_w = 1.0 + nrm((L, D_MODEL), 0.02)
    return {'x_prompt': x_prompt, 'x_sample': x_sample, 'mem_prompt': mem_prompt,
            'cache_mem_k': cache_mem_k, 'cache_mem_v': cache_mem_v,
            'state_conv_a': state_conv_a, 'state_conv_ssd': state_conv_ssd, 'state_ssm': state_ssm,
            'norm_pre_w': norm_pre_w, 'w_in': w_in, 'gate_b': gate_b,
            'conv_dw_w': conv_dw_w, 'conv_dw_b': conv_dw_b, 'conv_ln_w': conv_ln_w,
            'conv_ln_b': conv_ln_b, 'conv_out_w': conv_out_w,
            'ssd_conv_w': ssd_conv_w, 'ssd_conv_b': ssd_conv_b, 'ssd_dt_bias': ssd_dt_bias,
            'ssd_a_log': ssd_a_log, 'ssd_d': ssd_d, 'ssd_norm_w': ssd_norm_w, 'ssd_out_w': ssd_out_w,
            'mem_norm_w': mem_norm_w, 'xa_kv_w': xa_kv_w, 'xa_out_w': xa_out_w,
            'w_out': w_out, 'norm_post_w': norm_post_w}


def reference(x_prompt, x_sample, mem_prompt, cache_mem_k, cache_mem_v,
              state_conv_a, state_conv_ssd, state_ssm,
              norm_pre_w, w_in, gate_b, conv_dw_w, conv_dw_b, conv_ln_w, conv_ln_b, conv_out_w,
              ssd_conv_w, ssd_conv_b, ssd_dt_bias, ssd_a_log, ssd_d, ssd_norm_w, ssd_out_w,
              mem_norm_w, xa_kv_w, xa_out_w, w_out, norm_post_w):
    def lw(l):
        return (norm_pre_w[l], w_in[l], gate_b[l], conv_dw_w[l], conv_dw_b[l], conv_ln_w[l],
                conv_ln_b[l], conv_out_w[l], ssd_conv_w[l], ssd_conv_b[l], ssd_dt_bias[l],
                ssd_a_log[l], ssd_d[l], ssd_norm_w[l], ssd_out_w[l], xa_out_w[l], w_out[l],
                norm_post_w[l])

    bp = x_prompt.shape[0]
    dtp = x_prompt.dtype
    y_p = x_prompt
    mk_p, mv_p, ca_p, cs_p, hs_p = [], [], [], [], []
    for l in range(DEPTH):
        mk, mv = _mem_kv(mem_prompt, mem_norm_w[l], xa_kv_w[l])
        y_p, na, ns, nh = _layer(y_p, mk, mv,
                                 jnp.zeros((bp, CONV_WIDTH - 1, CONV_DIM), dtp),
                                 jnp.zeros((bp, SSD_CONV_WIDTH - 1, SSD_XBC), dtp),
                                 jnp.zeros((bp, SSD_HEADS, SSD_HEAD_DIM, SSD_STATE), dtp),
                                 *lw(l))
        mk_p.append(mk); mv_p.append(mv); ca_p.append(na); cs_p.append(ns); hs_p.append(nh)

    y_s = x_sample
    ca_s, cs_s, hs_s = [], [], []
    for l in range(DEPTH):
        y_s, na, ns, nh = _layer(y_s, cache_mem_k[l], cache_mem_v[l],
                                 state_conv_a[l], state_conv_ssd[l], state_ssm[l], *lw(l))
        ca_s.append(na); cs_s.append(ns); hs_s.append(nh)

    return (y_p, y_s,
            jnp.stack(mk_p), jnp.stack(mv_p), jnp.stack(ca_p), jnp.stack(cs_p), jnp.stack(hs_p),
            jnp.stack(ca_s), jnp.stack(cs_s), jnp.stack(hs_s))
```

```python
import functools

import jax
import jax.numpy as jnp
import numpy as np
from jax import lax
from jax.experimental import pallas as pl
from jax.experimental.pallas import tpu as pltpu

D_MODEL = 1024
DEPTH = 2
N_MEM = 256
CONV_DIM = D_MODEL
CONV_WIDTH = 31
D_INNER = 2 * D_MODEL
SSD_HEAD_DIM = 64
SSD_HEADS = D_INNER // SSD_HEAD_DIM
SSD_GROUPS = 4
SSD_HPG = SSD_HEADS // SSD_GROUPS
SSD_STATE = 128
SSD_CONV_WIDTH = 4
SSD_BC = SSD_GROUPS * SSD_STATE
SSD_XBC = D_INNER + 2 * SSD_BC
XA_HEADS = 4
XA_HEAD_DIM = D_MODEL // XA_HEADS
XA_DIM = XA_HEADS * XA_HEAD_DIM
N_BRANCH = 3
IN_SIZES = (CONV_DIM, CONV_DIM, CONV_DIM, D_INNER, SSD_XBC, SSD_HEADS, XA_DIM, XA_DIM, N_BRANCH * D_MODEL)
EPS = 1e-6

LANES = 128
SUBLANES = 8
SCAN_CHUNK = 64
CONV_HIST_ROWS = 32
SSD_HIST_ROWS = 8
CONV_ROW_BLOCK = 16
VMEM_LIMIT_BYTES = 56 * 1024 * 1024

F32 = jnp.float32
BF16 = jnp.bfloat16


def _time_tile(t):
    return 256 if t % 256 == 0 else t


def _rms(x, w):
    return x * lax.rsqrt(jnp.mean(x * x, axis=-1, keepdims=True) + EPS) * w


def _sigmoid(x):
    return 1.0 / (1.0 + jnp.exp(-x))


def _silu(x):
    return x * _sigmoid(x)


def _softplus(x):
    return jnp.maximum(x, 0.0) + jnp.log1p(jnp.exp(-jnp.abs(x)))


def _dot(a, b):
    return jnp.dot(a.astype(BF16), b.astype(BF16), preferred_element_type=F32)


def _dot_nt(a, b):
    return lax.dot_general(a.astype(BF16), b.astype(BF16), (((1,), (1,)), ((), ())),
                           preferred_element_type=F32)


def _dot_tn(a, b):
    return lax.dot_general(a.astype(BF16), b.astype(BF16), (((0,), (0,)), ((), ())),
                           preferred_element_type=F32)


def _split3(v):
    hi = v.astype(BF16)
    r1 = v - hi.astype(F32)
    mid = r1.astype(BF16)
    lo = (r1 - mid.astype(F32)).astype(BF16)
    return hi, mid, lo


def _dot_exact_rhs(a_bf16, v):
    hi, mid, lo = _split3(v)
    return (jnp.dot(a_bf16, hi, preferred_element_type=F32)
            + jnp.dot(a_bf16, mid, preferred_element_type=F32)
            + jnp.dot(a_bf16, lo, preferred_element_type=F32))


def _expand_heads(v, e_ref):
    hi, mid, lo = _split3(v)
    e = e_ref[...]
    return (jnp.dot(hi, e, preferred_element_type=F32)
            + jnp.dot(mid, e, preferred_element_type=F32)
            + jnp.dot(lo, e, preferred_element_type=F32))


def _memkv_kernel(mem_ref, nw_ref, w_ref, k_ref, v_ref):
    h = _rms(mem_ref[0], nw_ref[...])
    kv = _dot(h, w_ref[...])
    k_ref[0] = kv[:, :XA_DIM]
    v_ref[0] = kv[:, XA_DIM:]


def _mem_kv(mem, norm_w, kv_w):
    b = mem.shape[0]
    const = lambda i: (0, 0)
    return pl.pallas_call(
        _memkv_kernel,
        out_shape=(jax.ShapeDtypeStruct((b, N_MEM, XA_DIM), F32),
                   jax.ShapeDtypeStruct((b, N_MEM, XA_DIM), F32)),
        grid=(b,),
        in_specs=[pl.BlockSpec((1, N_MEM, D_MODEL), lambda i: (i, 0, 0)),
                  pl.BlockSpec((1, D_MODEL), const),
                  pl.BlockSpec((D_MODEL, 2 * XA_DIM), const, pipeline_mode=pl.Buffered(1))],
        out_specs=(pl.BlockSpec((1, N_MEM, XA_DIM), lambda i: (i, 0, 0)),
                   pl.BlockSpec((1, N_MEM, XA_DIM), lambda i: (i, 0, 0))),
        compiler_params=pltpu.CompilerParams(dimension_semantics=("arbitrary",),
                                             vmem_limit_bytes=VMEM_LIMIT_BYTES),
        name="mem_kv",
    )(mem, norm_w.reshape(1, D_MODEL), kv_w.astype(BF16))


def _conv_branch_kernel(x_ref, st_ref, npw_ref, w3_ref, wg_ref, gb_ref, cw_ref, cb_ref, lnw_ref, lnb_ref,
                        cow_ref, ma_ref, newa_ref, ubuf, cbuf, *, tt, nt):
    t = pl.program_id(1)

    @pl.when(t == 0)
    def _():
        ubuf[0:CONV_HIST_ROWS, :] = st_ref[0]

    hb = _rms(x_ref[0], npw_ref[...]).astype(BF16)
    glu_v = jnp.dot(hb, w3_ref[:, 0:CONV_DIM], preferred_element_type=F32)
    glu_g = jnp.dot(hb, w3_ref[:, CONV_DIM:2 * CONV_DIM], preferred_element_type=F32)
    ubuf[CONV_HIST_ROWS:CONV_HIST_ROWS + tt, :] = glu_v * _sigmoid(glu_g)

    first_tap = CONV_HIST_ROWS - (CONV_WIDTH - 1)
    for rb in range(tt // CONV_ROW_BLOCK):
        r0 = rb * CONV_ROW_BLOCK
        acc = jnp.broadcast_to(cb_ref[...], (CONV_ROW_BLOCK, CONV_DIM))
        for k in range(CONV_WIDTH):
            acc = acc + cw_ref[k:k + 1, :] * ubuf[r0 + first_tap + k:r0 + first_tap + k + CONV_ROW_BLOCK, :]
        cbuf[r0:r0 + CONV_ROW_BLOCK, :] = acc

    c = cbuf[...]
    mu = jnp.mean(c, axis=-1, keepdims=True)
    xc = c - mu
    y = xc * lax.rsqrt(jnp.mean(xc * xc, axis=-1, keepdims=True) + EPS) * lnw_ref[...] + lnb_ref[...]
    conv_gate = jnp.dot(hb, w3_ref[:, 2 * CONV_DIM:3 * CONV_DIM], preferred_element_type=F32)
    y = _silu(y) * _silu(conv_gate)
    out_a = _dot(y, cow_ref[...])
    gate = _sigmoid(jnp.dot(hb, wg_ref[...], preferred_element_type=F32) + gb_ref[...])
    ma_ref[0] = gate * out_a

    ubuf[0:CONV_HIST_ROWS, :] = ubuf[tt:tt + CONV_HIST_ROWS, :]

    @pl.when(t == nt - 1)
    def _():
        newa_ref[0] = ubuf[0:CONV_HIST_ROWS, :]


def _conv_branch(x, st_a, npw, w3, wg, gb, cw, cb, lnw, lnb, cow):
    b, t, _ = x.shape
    tt = _time_tile(t)
    nt = t // tt
    assert tt >= CONV_HIST_ROWS and tt % CONV_ROW_BLOCK == 0
    const = lambda i, j: (0, 0)
    wspec = lambda shape: pl.BlockSpec(shape, const, pipeline_mode=pl.Buffered(1))
    st_pad = jnp.pad(st_a, ((0, 0), (CONV_HIST_ROWS - (CONV_WIDTH - 1), 0), (0, 0)))
    ma, newa = pl.pallas_call(
        functools.partial(_conv_branch_kernel, tt=tt, nt=nt),
        out_shape=(jax.ShapeDtypeStruct((b, t, D_MODEL), F32),
                   jax.ShapeDtypeStruct((b, CONV_HIST_ROWS, CONV_DIM), F32)),
        grid=(b, nt),
        in_specs=[pl.BlockSpec((1, tt, D_MODEL), lambda i, j: (i, j, 0)),
                  pl.BlockSpec((1, CONV_HIST_ROWS, CONV_DIM), lambda i, j: (i, 0, 0)),
                  wspec((1, D_MODEL)),
                  wspec((D_MODEL, 3 * CONV_DIM)),
                  wspec((D_MODEL, D_MODEL)),
                  wspec((1, D_MODEL)),
                  wspec((CONV_WIDTH, CONV_DIM)),
                  wspec((1, CONV_DIM)),
                  wspec((1, CONV_DIM)),
                  wspec((1, CONV_DIM)),
                  wspec((CONV_DIM, D_MODEL))],
        out_specs=(pl.BlockSpec((1, tt, D_MODEL), lambda i, j: (i, j, 0)),
                   pl.BlockSpec((1, CONV_HIST_ROWS, CONV_DIM), lambda i, j: (i, 0, 0))),
        scratch_shapes=[pltpu.VMEM((tt + CONV_HIST_ROWS, CONV_DIM), F32),
                        pltpu.VMEM((tt, CONV_DIM), F32)],
        compiler_params=pltpu.CompilerParams(dimension_semantics=("arbitrary", "arbitrary"),
                                             vmem_limit_bytes=VMEM_LIMIT_BYTES),
        name="conv_branch",
    )(x, st_pad, npw, w3, wg, gb, cw, cb, lnw, lnb, cow)
    return ma, newa[:, CONV_HIST_ROWS - (CONV_WIDTH - 1):, :]


def _ssd_branch_kernel(x_ref, sst_ref, sh_ref, npw_ref, wz_ref, wxbc_ref, wdt_ref, wg_ref, gb_ref,
                       scw_ref, scb_ref, dtb_ref, alog_ref, dfull_ref, nw_ref, wout_ref, e_ref,
                       mb_ref, newssd_ref, newh_ref,
                       xbuf, state, xs_buf, bc_buf, dt_buf, da_buf, y_buf, *, tt, nt):
    t = pl.program_id(1)
    L = SCAN_CHUNK

    @pl.when(t == 0)
    def _():
        xbuf[0:SSD_HIST_ROWS, :] = sst_ref[0]
        state[...] = sh_ref[0]

    hb = _rms(x_ref[0], npw_ref[...]).astype(BF16)
    xbuf[SSD_HIST_ROWS:SSD_HIST_ROWS + tt, :] = jnp.dot(hb, wxbc_ref[...], preferred_element_type=F32)

    first_tap = SSD_HIST_ROWS - (SSD_CONV_WIDTH - 1)
    acc = jnp.broadcast_to(scb_ref[...], (tt, SSD_XBC))
    for k in range(SSD_CONV_WIDTH):
        acc = acc + scw_ref[k:k + 1, :] * xbuf[first_tap + k:first_tap + k + tt, :]
    acc = _silu(acc)
    xs_buf[...] = acc[:, :D_INNER]
    bc_buf[...] = acc[:, D_INNER:]

    dt = _softplus(jnp.dot(hb, wdt_ref[...], preferred_element_type=F32) + dtb_ref[...])
    dt_buf[...] = dt
    da_buf[...] = dt * (-jnp.exp(alog_ref[...]))

    row = lax.broadcasted_iota(jnp.int32, (L, L), 0)
    col = lax.broadcasted_iota(jnp.int32, (L, L), 1)
    causal = row >= col
    tril = jnp.where(causal, 1.0, 0.0).astype(BF16)

    def chunk_body(ci, carry):
        r0 = pl.multiple_of(ci * L, L)
        rows = pl.ds(r0, L)
        acs = _dot_exact_rhs(tril, da_buf[rows, :])
        dtc = dt_buf[rows, :]
        total = acs[L - 1:L, :]
        acs_t = jnp.transpose(acs)
        dt_t = jnp.transpose(dtc)
        dfs_x = _expand_heads(jnp.exp(acs), e_ref)
        w_x = _expand_heads(jnp.exp(total - acs) * dtc, e_ref)
        cd_x = _expand_heads(jnp.broadcast_to(jnp.exp(total), (SUBLANES, LANES)), e_ref)[0:1, :]
        xs = xs_buf[rows, :]
        xw = xs * w_x
        for g in range(SSD_GROUPS):
            gs = slice(g * SSD_HPG * SSD_HEAD_DIM, (g + 1) * SSD_HPG * SSD_HEAD_DIM)
            bg = bc_buf[rows, g * SSD_STATE:(g + 1) * SSD_STATE]
            cg = bc_buf[rows, SSD_BC + g * SSD_STATE:SSD_BC + (g + 1) * SSD_STATE]
            cb = _dot_nt(cg, bg)
            sg = state[:, gs]
            y_off = _dot(cg, sg) * dfs_x[:, gs]
            for e in range(SSD_HPG):
                h = g * SSD_HPG + e
                hs = slice(h * SSD_HEAD_DIM, (h + 1) * SSD_HEAD_DIM)
                lm = jnp.where(causal, jnp.exp(acs[:, h:h + 1] - acs_t[h:h + 1, :]), 0.0)
                gm = cb * lm * dt_t[h:h + 1, :]
                y_buf[rows, hs] = _dot(gm, xs[:, hs]) + y_off[:, e * SSD_HEAD_DIM:(e + 1) * SSD_HEAD_DIM]
            state[:, gs] = sg * cd_x[:, gs] + _dot_tn(bg, xw[:, gs])
        return carry

    lax.fori_loop(0, tt // L, chunk_body, 0)

    y = y_buf[...] + xs_buf[...] * dfull_ref[...]
    y = y * _silu(jnp.dot(hb, wz_ref[...], preferred_element_type=F32))
    gw = D_INNER // SSD_GROUPS
    parts = []
    for g in range(SSD_GROUPS):
        yg = y[:, g * gw:(g + 1) * gw]
        parts.append(yg * lax.rsqrt(jnp.mean(yg * yg, axis=-1, keepdims=True) + EPS))
    y = jnp.concatenate(parts, axis=-1) * nw_ref[...]
    out_b = _dot(y, wout_ref[...])
    gate = _sigmoid(jnp.dot(hb, wg_ref[...], preferred_element_type=F32) + gb_ref[...])
    mb_ref[0] = gate * out_b

    xbuf[0:SSD_HIST_ROWS, :] = xbuf[tt:tt + SSD_HIST_ROWS, :]

    @pl.when(t == nt - 1)
    def _():
        newssd_ref[0] = xbuf[0:SSD_HIST_ROWS, :]
        newh_ref[0] = state[...]


def _head_expand_matrix():
    e = np.zeros((LANES, D_INNER), np.float32)
    for h in range(SSD_HEADS):
        e[h, h * SSD_HEAD_DIM:(h + 1) * SSD_HEAD_DIM] = 1.0
    return jnp.asarray(e, BF16)


def _ssd_branch(x, st_ssd, st_h, npw, wz, wxbc, wdt, wg, gb, scw, scb, dtb, alog, dfull, nw, wout):
    b, t, _ = x.shape
    tt = _time_tile(t)
    nt = t // tt
    assert tt % SCAN_CHUNK == 0
    const = lambda i, j: (0, 0)
    wspec = lambda shape: pl.BlockSpec(shape, const, pipeline_mode=pl.Buffered(1))
    sst_pad = jnp.pad(st_ssd, ((0, 0), (SSD_HIST_ROWS - (SSD_CONV_WIDTH - 1), 0), (0, 0)))
    sh_t = jnp.transpose(st_h, (0, 3, 1, 2)).reshape(b, SSD_STATE, D_INNER)
    mb, newssd, newh = pl.pallas_call(
        functools.partial(_ssd_branch_kernel, tt=tt, nt=nt),
        out_shape=(jax.ShapeDtypeStruct((b, t, D_MODEL), F32),
                   jax.ShapeDtypeStruct((b, SSD_HIST_ROWS, SSD_XBC), F32),
                   jax.ShapeDtypeStruct((b, SSD_STATE, D_INNER), F32)),
        grid=(b, nt),
        in_specs=[pl.BlockSpec((1, tt, D_MODEL), lambda i, j: (i, j, 0)),
                  pl.BlockSpec((1, SSD_HIST_ROWS, SSD_XBC), lambda i, j: (i, 0, 0)),
                  pl.BlockSpec((1, SSD_STATE, D_INNER), lambda i, j: (i, 0, 0)),
                  wspec((1, D_MODEL)),
                  wspec((D_MODEL, D_INNER)),
                  wspec((D_MODEL, SSD_XBC)),
                  wspec((D_MODEL, LANES)),
                  wspec((D_MODEL, D_MODEL)),
                  wspec((1, D_MODEL)),
                  wspec((SSD_CONV_WIDTH, SSD_XBC)),
                  wspec((1, SSD_XBC)),
                  wspec((1, LANES)),
                  wspec((1, LANES)),
                  wspec((1, D_INNER)),
                  wspec((1, D_INNER)),
                  wspec((D_INNER, D_MODEL)),
                  wspec((LANES, D_INNER))],
        out_specs=(pl.BlockSpec((1, tt, D_MODEL), lambda i, j: (i, j, 0)),
                   pl.BlockSpec((1, SSD_HIST_ROWS, SSD_XBC), lambda i, j: (i, 0, 0)),
                   pl.BlockSpec((1, SSD_STATE, D_INNER), lambda i, j: (i, 0, 0))),
        scratch_shapes=[pltpu.VMEM((tt + SSD_HIST_ROWS, SSD_XBC), F32),
                        pltpu.VMEM((SSD_STATE, D_INNER), F32),
                        pltpu.VMEM((tt, D_INNER), F32),
                        pltpu.VMEM((tt, 2 * SSD_BC), F32),
                        pltpu.VMEM((tt, LANES), F32),
                        pltpu.VMEM((tt, LANES), F32),
                        pltpu.VMEM((tt, D_INNER), F32)],
        compiler_params=pltpu.CompilerParams(dimension_semantics=("arbitrary", "arbitrary"),
                                             vmem_limit_bytes=VMEM_LIMIT_BYTES),
        name="ssd_branch",
    )(x, sst_pad, sh_t, npw, wz, wxbc, wdt, wg, gb, scw, scb, dtb, alog, dfull, nw, wout, _head_expand_matrix())
    new_h = jnp.transpose(newh.reshape(b, SSD_STATE, SSD_HEADS, SSD_HEAD_DIM), (0, 2, 3, 1))
    return mb, newssd[:, SSD_HIST_ROWS - (SSD_CONV_WIDTH - 1):, :], new_h


def _attn_merge_kernel(x_ref, ma_ref, mb_ref, k_ref, v_ref, npw_ref, wq_ref, wxg_ref, wg_ref, gb_ref,
                       xow_ref, wo_ref, npost_ref, y_ref):
    x = x_ref[0]
    hb = _rms(x, npw_ref[...]).astype(BF16)
    q = jnp.dot(hb, wq_ref[...], preferred_element_type=F32)
    kb = k_ref[0].astype(BF16)
    vb = v_ref[0].astype(BF16)
    heads = []
    for hd in range(XA_HEADS):
        sl = slice(hd * XA_HEAD_DIM, (hd + 1) * XA_HEAD_DIM)
        s = _dot_nt(q[:, sl], kb[:, sl]) * (XA_HEAD_DIM ** -0.5)
        e = jnp.exp(s - jnp.max(s, axis=-1, keepdims=True))
        p = e / jnp.sum(e, axis=-1, keepdims=True)
        heads.append(_dot(p, vb[:, sl]))
    o = jnp.concatenate(heads, axis=-1)
    o = o * _silu(jnp.dot(hb, wxg_ref[...], preferred_element_type=F32))
    out_c = _dot(o, xow_ref[...])
    gate = _sigmoid(jnp.dot(hb, wg_ref[...], preferred_element_type=F32) + gb_ref[...])
    m = ma_ref[0] + mb_ref[0] + gate * out_c
    y_ref[0] = x + _rms(_dot(m, wo_ref[...]), npost_ref[...])


def _attn_merge(x, ma, mb, mem_k, mem_v, npw, wq, wxg, wg, gb, xow, wo, npost):
    b, t, _ = x.shape
    tt = _time_tile(t)
    nt = t // tt
    const = lambda i, j: (0, 0)
    wspec = lambda shape: pl.BlockSpec(shape, const, pipeline_mode=pl.Buffered(1))
    tile = pl.BlockSpec((1, tt, D_MODEL), lambda i, j: (i, j, 0))
    mem = pl.BlockSpec((1, N_MEM, XA_DIM), lambda i, j: (i, 0, 0))
    return pl.pallas_call(
        _attn_merge_kernel,
        out_shape=jax.ShapeDtypeStruct((b, t, D_MODEL), F32),
        grid=(b, nt),
        in_specs=[tile, tile, tile, mem, mem,
                  wspec((1, D_MODEL)),
                  wspec((D_MODEL, XA_DIM)),
                  wspec((D_MODEL, XA_DIM)),
                  wspec((D_MODEL, D_MODEL)),
                  wspec((1, D_MODEL)),
                  wspec((XA_DIM, D_MODEL)),
                  wspec((D_MODEL, D_MODEL)),
                  wspec((1, D_MODEL))],
        out_specs=tile,
        compiler_params=pltpu.CompilerParams(dimension_semantics=("arbitrary", "arbitrary"),
                                             vmem_limit_bytes=VMEM_LIMIT_BYTES),
        name="attn_merge",
    )(x, ma, mb, mem_k, mem_v, npw, wq, wxg, wg, gb, xow, wo, npost)


def _layer_weights(l, norm_pre_w, w_in, gate_b, conv_dw_w, conv_dw_b, conv_ln_w, conv_ln_b, conv_out_w,
                   ssd_conv_w, ssd_conv_b, ssd_dt_bias, ssd_a_log, ssd_d, ssd_norm_w, ssd_out_w,
                   xa_out_w, w_out, norm_post_w):
    offs = np.cumsum((0,) + IN_SIZES)
    seg = lambda i: w_in[l][:, offs[i]:offs[i + 1]].astype(BF16)
    row = lambda v: v.reshape(1, -1)
    pad_lanes = lambda v: jnp.pad(v, ((0, 0), (0, LANES - v.shape[-1])))
    gates_w = seg(8)
    gb = gate_b[l]
    return dict(
        npw=row(norm_pre_w[l]),
        w3=w_in[l][:, offs[0]:offs[3]].astype(BF16),
        wz=seg(3), wxbc=seg(4), wdt=pad_lanes(seg(5)), wq=seg(6), wxg=seg(7),
        wg=[gates_w[:, i * D_MODEL:(i + 1) * D_MODEL] for i in range(N_BRANCH)],
        gb=[row(gb[i * D_MODEL:(i + 1) * D_MODEL]) for i in range(N_BRANCH)],
        cw=conv_dw_w[l], cb=row(conv_dw_b[l]), lnw=row(conv_ln_w[l]), lnb=row(conv_ln_b[l]),
        cow=conv_out_w[l].astype(BF16),
        scw=ssd_conv_w[l], scb=row(ssd_conv_b[l]),
        dtb=pad_lanes(row(ssd_dt_bias[l])), alog=pad_lanes(row(ssd_a_log[l])),
        dfull=row(jnp.repeat(ssd_d[l], SSD_HEAD_DIM)), nw=row(ssd_norm_w[l]),
        wout=ssd_out_w[l].astype(BF16),
        xow=xa_out_w[l].astype(BF16), wo=w_out[l].astype(BF16), npost=row(norm_post_w[l]),
    )


def _layer(x, mem_k, mem_v, st_a, st_ssd, st_h, w):
    ma, new_a = _conv_branch(x, st_a, w["npw"], w["w3"], w["wg"][0], w["gb"][0], w["cw"], w["cb"],
                             w["lnw"], w["lnb"], w["cow"])
    mb, new_ssd, new_h = _ssd_branch(x, st_ssd, st_h, w["npw"], w["wz"], w["wxbc"], w["wdt"], w["wg"][1],
                                     w["gb"][1], w["scw"], w["scb"], w["dtb"], w["alog"], w["dfull"],
                                     w["nw"], w["wout"])
    y = _attn_merge(x, ma, mb, mem_k, mem_v, w["npw"], w["wq"], w["wxg"], w["wg"][2], w["gb"][2],
                    w["xow"], w["wo"], w["npost"])
    return y, new_a, new_ssd, new_h


def kernel(x_prompt, x_sample, mem_prompt, cache_mem_k, cache_mem_v, state_conv_a, state_conv_ssd, state_ssm,
           norm_pre_w, w_in, gate_b, conv_dw_w, conv_dw_b, conv_ln_w, conv_ln_b, conv_out_w,
           ssd_conv_w, ssd_conv_b, ssd_dt_bias, ssd_a_log, ssd_d, ssd_norm_w, ssd_out_w,
           mem_norm_w, xa_kv_w, xa_out_w, w_out, norm_post_w):
    bp = x_prompt.shape[0]
    bs = x_sample.shape[0]
    weights = [_layer_weights(l, norm_pre_w, w_in, gate_b, conv_dw_w, conv_dw_b, conv_ln_w, conv_ln_b,
                              conv_out_w, ssd_conv_w, ssd_conv_b, ssd_dt_bias, ssd_a_log, ssd_d, ssd_norm_w,
                              ssd_out_w, xa_out_w, w_out, norm_post_w) for l in range(DEPTH)]

    y_p = x_prompt
    mk_p, mv_p, ca_p, cs_p, hs_p = [], [], [], [], []
    for l in range(DEPTH):
        mk, mv = _mem_kv(mem_prompt, mem_norm_w[l], xa_kv_w[l])
        y_p, na, ns, nh = _layer(y_p, mk, mv,
                                 jnp.zeros((bp, CONV_WIDTH - 1, CONV_DIM), F32),
                                 jnp.zeros((bp, SSD_CONV_WIDTH - 1, SSD_XBC), F32),
                                 jnp.zeros((bp, SSD_HEADS, SSD_HEAD_DIM, SSD_STATE), F32),
                                 weights[l])
        mk_p.append(mk.reshape(bp, N_MEM, XA_HEADS, XA_HEAD_DIM))
        mv_p.append(mv.reshape(bp, N_MEM, XA_HEADS, XA_HEAD_DIM))
        ca_p.append(na); cs_p.append(ns); hs_p.append(nh)

    y_s = x_sample
    ca_s, cs_s, hs_s = [], [], []
    for l in range(DEPTH):
        y_s, na, ns, nh = _layer(y_s,
                                 cache_mem_k[l].reshape(bs, N_MEM, XA_DIM),
                                 cache_mem_v[l].reshape(bs, N_MEM, XA_DIM),
                                 state_conv_a[l], state_conv_ssd[l], state_ssm[l], weights[l])
        ca_s.append(na); cs_s.append(ns); hs_s.append(nh)

    return (y_p, y_s,
            jnp.stack(mk_p), jnp.stack(mv_p), jnp.stack(ca_p), jnp.stack(cs_p), jnp.stack(hs_p),
            jnp.stack(ca_s), jnp.stack(cs_s), jnp.stack(hs_s))
```

```python
import functools

import jax
import jax.numpy as jnp
import numpy as np
from jax import lax
from jax.experimental import pallas as pl
from jax.experimental.pallas import tpu as pltpu

D_MODEL = 1024
DEPTH = 2
N_MEM = 256
CONV_DIM = D_MODEL
CONV_WIDTH = 31
D_INNER = 2 * D_MODEL
SSD_HEAD_DIM = 64
SSD_HEADS = D_INNER // SSD_HEAD_DIM
SSD_GROUPS = 4
SSD_HPG = SSD_HEADS // SSD_GROUPS
SSD_STATE = 128
SSD_CONV_WIDTH = 4
SSD_BC = SSD_GROUPS * SSD_STATE
SSD_XBC = D_INNER + 2 * SSD_BC
XA_HEADS = 4
XA_HEAD_DIM = D_MODEL // XA_HEADS
XA_DIM = XA_HEADS * XA_HEAD_DIM
N_BRANCH = 3
IN_SIZES = (CONV_DIM, CONV_DIM, CONV_DIM, D_INNER, SSD_XBC, SSD_HEADS, XA_DIM, XA_DIM, N_BRANCH * D_MODEL)
EPS = 1e-6

LANES = 128
SUBLANES = 8
SCAN_CHUNK = 64
CONV_HIST_ROWS = 32
SSD_HIST_ROWS = 8
CONV_ROW_BLOCK = 64
CONV_TAP_GROUP = 8
VMEM_LIMIT_BYTES =56 * 1024 * 1024

F32 = jnp.float32
BF16 = jnp.bfloat16


def _time_tile(t):
    return 256 if t % 256 == 0 else t


def _rms(x, w):
    return x * lax.rsqrt(jnp.mean(x * x, axis=-1, keepdims=True) + EPS) * w


def _sigmoid(x):
    return 1.0 / (1.0 + jnp.exp(-x))


def _silu(x):
    return x * _sigmoid(x)


def _softplus(x):
    return jnp.maximum(x, 0.0) + jnp.log1p(jnp.exp(-jnp.abs(x)))


def _dot(a, b):
    return jnp.dot(a.astype(BF16), b.astype(BF16), preferred_element_type=F32)


def _dot_nt(a, b):
    return lax.dot_general(a.astype(BF16), b.astype(BF16), (((1,), (1,)), ((), ())),
                           preferred_element_type=F32)


def _dot_tn(a, b):
    return lax.dot_general(a.astype(BF16), b.astype(BF16), (((0,), (0,)), ((), ())),
                           preferred_element_type=F32)


def _split3(v):
    hi = v.astype(BF16)
    r1 = v - hi.astype(F32)
    mid = r1.astype(BF16)
    lo = (r1 - mid.astype(F32)).astype(BF16)
    return hi, mid, lo


def _dot_exact_rhs(a_bf16, v):
    hi, mid, lo = _split3(v)
    return (jnp.dot(a_bf16, hi, preferred_element_type=F32)
            + jnp.dot(a_bf16, mid, preferred_element_type=F32)
            + jnp.dot(a_bf16, lo, preferred_element_type=F32))


def _expand_heads(v, e_ref):
    hi, mid, lo = _split3(v)
    e = e_ref[...]
    return (jnp.dot(hi, e, preferred_element_type=F32)
            + jnp.dot(mid, e, preferred_element_type=F32)
            + jnp.dot(lo, e, preferred_element_type=F32))


def _memkv_kernel(mem_ref, nw_ref, w_ref, k_ref, v_ref):
    h = _rms(mem_ref[0], nw_ref[...])
    kv = _dot(h, w_ref[...])
    k_ref[0] = kv[:, :XA_DIM]
    v_ref[0] = kv[:, XA_DIM:]


def _mem_kv(mem, norm_w, kv_w):
    b = mem.shape[0]
    const = lambda i: (0, 0)
    return pl.pallas_call(
        _memkv_kernel,
        out_shape=(jax.ShapeDtypeStruct((b, N_MEM, XA_DIM), F32),
                   jax.ShapeDtypeStruct((b, N_MEM, XA_DIM), F32)),
        grid=(b,),
        in_specs=[pl.BlockSpec((1, N_MEM, D_MODEL), lambda i: (i, 0, 0)),
                  pl.BlockSpec((1, D_MODEL), const),
                  pl.BlockSpec((D_MODEL, 2 * XA_DIM), const, pipeline_mode=pl.Buffered(1))],
        out_specs=(pl.BlockSpec((1, N_MEM, XA_DIM), lambda i: (i, 0, 0)),
                   pl.BlockSpec((1, N_MEM, XA_DIM), lambda i: (i, 0, 0))),
        compiler_params=pltpu.CompilerParams(dimension_semantics=("arbitrary",),
                                             vmem_limit_bytes=VMEM_LIMIT_BYTES),
        name="mem_kv",
    )(mem, norm_w.reshape(1, D_MODEL), kv_w.astype(BF16))


def _time_permutation(tt):
    n = tt // SUBLANES
    perm = np.zeros((tt, tt), np.float32)
    hist = np.zeros((tt, CONV_HIST_ROWS), np.float32)
    for rho in range(tt):
        i, s = divmod(rho, SUBLANES)
        perm[rho, s * n + i] = 1.0
    for m in range(CONV_HIST_ROWS):
        s = SUBLANES - CONV_HIST_ROWS // n + m // n
        hist[(m % n) * SUBLANES + s, m] = 1.0
    return perm, hist


def _conv_branch_kernel(x_ref, st_ref, perm_ref, permt_ref, hist_ref, histt_ref, npw_ref, w3_ref, wg_ref, gb_ref,
                        cw_ref, cb_ref, lnw_ref, lnb_ref, cow_ref, ma_ref, newa_ref, ext, cbuf, *, tt, nt, wraps):
    t = pl.program_id(1)
    n = tt // SUBLANES
    cur = wraps * tt

    nslab = CONV_DIM // LANES

    def put(ref, r0, val):
        for c in range(nslab):
            ref[c, r0:r0 + val.shape[0], :] = val[:, c * LANES:(c + 1) * LANES]

    def get(ref, r0, rows):
        return jnp.concatenate([ref[c, r0:r0 + rows, :] for c in range(nslab)], axis=-1)

    @pl.when(t == 0)
    def _():
        put(ext, cur, _dot_exact_rhs(hist_ref[...], st_ref[0]))

    hb = _rms(x_ref[0], npw_ref[...]).astype(BF16)
    hb_p = jnp.dot(perm_ref[...], hb, preferred_element_type=F32).astype(BF16)
    glu_v = jnp.dot(hb_p, w3_ref[:, 0:CONV_DIM], preferred_element_type=F32)
    glu_g = jnp.dot(hb_p, w3_ref[:, CONV_DIM:2 * CONV_DIM], preferred_element_type=F32)
    u = glu_v * _sigmoid(glu_g)

    prev3 = get(ext, cur, tt).reshape(n, SUBLANES, CONV_DIM)
    u3 = u.reshape(n, SUBLANES, CONV_DIM)
    sub = lax.broadcasted_iota(jnp.int32, (n, SUBLANES, CONV_DIM), 1)
    for q in range(1, wraps + 1):
        mixed = jnp.where(sub >= SUBLANES - q, prev3, u3)
        put(ext, (wraps - q) * tt, pltpu.roll(mixed, q, 1).reshape(tt, CONV_DIM))
    put(ext, cur, u)

    vregs_per_block = CONV_ROW_BLOCK // SUBLANES
    blocks_per_slab = tt // CONV_ROW_BLOCK

    def conv_block(idx, carry):
        c = idx // blocks_per_slab
        r0 = (idx % blocks_per_slab) * CONV_ROW_BLOCK
        first = cur + r0 - SUBLANES * (CONV_WIDTH - 1)
        acc = [jnp.broadcast_to(cb_ref[c], (SUBLANES, LANES))] * vregs_per_block
        for k0 in range(0, CONV_WIDTH, CONV_TAP_GROUP):
            taps = range(k0, min(k0 + CONV_TAP_GROUP, CONV_WIDTH))
            wk = [jnp.broadcast_to(cw_ref[c, k:k + 1, :], (SUBLANES, LANES)) for k in taps]
            win = [ext[c, pl.ds(pl.multiple_of(first + SUBLANES * (k0 + j), SUBLANES), SUBLANES), :]
                   for j in range(vregs_per_block + len(taps) - 1)]
            for m in range(vregs_per_block):
                for j in range(len(taps)):
                    acc[m] = acc[m] + wk[j] * win[m + j]
        for m in range(vregs_per_block):
            cbuf[c, pl.ds(pl.multiple_of(r0 + SUBLANES * m, SUBLANES), SUBLANES), :] = acc[m]
        return carry

    lax.fori_loop(0, nslab * blocks_per_slab, conv_block, 0)

    c = get(cbuf, 0, tt)
    mu = jnp.mean(c, axis=-1, keepdims=True)
    xc = c - mu
    y = xc * lax.rsqrt(jnp.mean(xc * xc, axis=-1, keepdims=True) + EPS) * lnw_ref[...] + lnb_ref[...]
    conv_gate = jnp.dot(hb_p, w3_ref[:, 2 * CONV_DIM:3 * CONV_DIM], preferred_element_type=F32)
    y = (_silu(y) * _silu(conv_gate)).astype(BF16)
    y = jnp.dot(permt_ref[...], y, preferred_element_type=F32).astype(BF16)
    out_a = jnp.dot(y, cow_ref[...], preferred_element_type=F32)
    gate = _sigmoid(jnp.dot(hb, wg_ref[...], preferred_element_type=F32) + gb_ref[...])
    ma_ref[0] = gate * out_a

    @pl.when(t == nt - 1)
    def _():
        newa_ref[0] = _dot_exact_rhs(histt_ref[...], get(ext, cur, tt))


def _conv_branch(x, st_a, npw, w3, wg, gb, cw, cb, lnw, lnb, cow):
    b, t, _ = x.shape
    tt = _time_tile(t)
    nt = t // tt
    n = tt // SUBLANES
    assert tt % SUBLANES == 0 and CONV_HIST_ROWS % n == 0 and CONV_HIST_ROWS // n <= SUBLANES
    wraps = -(-(CONV_WIDTH - 1) // n)
    assert wraps < SUBLANES and wraps * tt >= SUBLANES * (CONV_WIDTH - 1)
    perm, hist = _time_permutation(tt)
    nslab = CONV_DIM // LANES
    assert tt % CONV_ROW_BLOCK == 0
    const = lambda i, j: (0, 0)
    wspec = lambda shape: pl.BlockSpec(shape, const, pipeline_mode=pl.Buffered(1))
    st_pad = jnp.pad(st_a, ((0, 0), (CONV_HIST_ROWS - (CONV_WIDTH - 1), 0), (0, 0)))
    ma, newa = pl.pallas_call(
        functools.partial(_conv_branch_kernel, tt=tt, nt=nt, wraps=wraps),
        out_shape=(jax.ShapeDtypeStruct((b, t, D_MODEL), F32),
                   jax.ShapeDtypeStruct((b, CONV_HIST_ROWS, CONV_DIM), F32)),
        grid=(b, nt),
        in_specs=[pl.BlockSpec((1, tt, D_MODEL), lambda i, j: (i, j, 0)),
                  pl.BlockSpec((1, CONV_HIST_ROWS, CONV_DIM), lambda i, j: (i, 0, 0)),
                  wspec((tt, tt)),
                  wspec((tt, tt)),
                  wspec((tt, CONV_HIST_ROWS)),
                  wspec((CONV_HIST_ROWS, tt)),
                  wspec((1, D_MODEL)),
                  wspec((D_MODEL, 3 * CONV_DIM)),
                  wspec((D_MODEL, D_MODEL)),
                  wspec((1, D_MODEL)),
                  pl.BlockSpec((nslab, CONV_WIDTH, LANES), lambda i, j: (0, 0, 0), pipeline_mode=pl.Buffered(1)),
                  pl.BlockSpec((nslab, 1, LANES), lambda i, j: (0, 0, 0), pipeline_mode=pl.Buffered(1)),
                  wspec((1, CONV_DIM)),
                  wspec((1, CONV_DIM)),
                  wspec((CONV_DIM, D_MODEL))],
        out_specs=(pl.BlockSpec((1, tt, D_MODEL), lambda i, j: (i, j, 0)),
                   pl.BlockSpec((1, CONV_HIST_ROWS, CONV_DIM), lambda i, j: (i, 0, 0))),
        scratch_shapes=[pltpu.VMEM((nslab, (wraps + 1) * tt, LANES), F32),
                        pltpu.VMEM((nslab, tt, LANES), F32)],
        compiler_params=pltpu.CompilerParams(dimension_semantics=("arbitrary", "arbitrary"),
                                             vmem_limit_bytes=VMEM_LIMIT_BYTES),
        name="conv_branch",
    )(x, st_pad, jnp.asarray(perm, BF16), jnp.asarray(perm.T, BF16), jnp.asarray(hist, BF16),
      jnp.asarray(hist.T, BF16), npw, w3, wg, gb,
      jnp.transpose(cw.reshape(CONV_WIDTH, nslab, LANES), (1, 0, 2)), cb.reshape(nslab, 1, LANES),
      lnw, lnb, cow)
    return ma, newa[:, CONV_HIST_ROWS - (CONV_WIDTH - 1):, :]


def _ssd_branch_kernel(x_ref, sst_ref, sh_ref, npw_ref, wz_ref, wxbc_ref, wdt_ref, wg_ref, gb_ref,
                       scw_ref, scb_ref, dtb_ref, alog_ref, dfull_ref, nw_ref, wout_ref, e_ref,
                       mb_ref, newssd_ref, newh_ref,
                       xbuf, state, xs_buf, bc_buf, dt_buf, da_buf, y_buf, *, tt, nt):
    t = pl.program_id(1)
    L = SCAN_CHUNK

    @pl.when(t == 0)
    def _():
        xbuf[0:SSD_HIST_ROWS, :] = sst_ref[0]
        state[...] = sh_ref[0]

    hb = _rms(x_ref[0], npw_ref[...]).astype(BF16)
    xbuf[SSD_HIST_ROWS:SSD_HIST_ROWS + tt, :] = jnp.dot(hb, wxbc_ref[...], preferred_element_type=F32)

    first_tap = SSD_HIST_ROWS - (SSD_CONV_WIDTH - 1)
    acc = jnp.broadcast_to(scb_ref[...], (tt, SSD_XBC))
    for k in range(SSD_CONV_WIDTH):
        acc = acc + scw_ref[k:k + 1, :] * xbuf[first_tap + k:first_tap + k + tt, :]
    acc = _silu(acc)
    xs_buf[...] = acc[:, :D_INNER]
    bc_buf[...] = acc[:, D_INNER:]

    dt = _softplus(jnp.dot(hb, wdt_ref[...], preferred_element_type=F32) + dtb_ref[...])
    dt_buf[...] = dt
    da_buf[...] = dt * (-jnp.exp(alog_ref[...]))

    row = lax.broadcasted_iota(jnp.int32, (L, L), 0)
    col = lax.broadcasted_iota(jnp.int32, (L, L), 1)
    causal = row >= col
    tril = jnp.where(causal, 1.0, 0.0).astype(BF16)

    def chunk_body(ci, carry):
        r0 = pl.multiple_of(ci * L, L)
        rows = pl.ds(r0, L)
        acs = _dot_exact_rhs(tril, da_buf[rows, :])
        dtc = dt_buf[rows, :]
        total = acs[L - 1:L, :]
        acs_t = jnp.transpose(acs)
        dt_t = jnp.transpose(dtc)
        dfs_x = _expand_heads(jnp.exp(acs), e_ref)
        w_x = _expand_heads(jnp.exp(total - acs) * dtc, e_ref)
        cd_x = _expand_heads(jnp.broadcast_to(jnp.exp(total), (SUBLANES, LANES)), e_ref)[0:1, :]
        xs = xs_buf[rows, :]
        xw = xs * w_x
        for g in range(SSD_GROUPS):
            gs = slice(g * SSD_HPG * SSD_HEAD_DIM, (g + 1) * SSD_HPG * SSD_HEAD_DIM)
            bg = bc_buf[rows, g * SSD_STATE:(g + 1) * SSD_STATE]
            cg = bc_buf[rows, SSD_BC + g * SSD_STATE:SSD_BC + (g + 1) * SSD_STATE]
            cb = _dot_nt(cg, bg)
            sg = state[:, gs]
            y_off = _dot(cg, sg) * dfs_x[:, gs]
            for e in range(SSD_HPG):
                h = g * SSD_HPG + e
                hs = slice(h * SSD_HEAD_DIM, (h + 1) * SSD_HEAD_DIM)
                lm = jnp.where(causal, jnp.exp(acs[:, h:h + 1] - acs_t[h:h + 1, :]), 0.0)
                gm = cb * lm * dt_t[h:h + 1, :]
                y_buf[rows, hs] = _dot(gm, xs[:, hs]) + y_off[:, e * SSD_HEAD_DIM:(e + 1) * SSD_HEAD_DIM]
            state[:, gs] = sg * cd_x[:, gs] + _dot_tn(bg, xw[:, gs])
        return carry

    lax.fori_loop(0, tt // L, chunk_body, 0)

    y = y_buf[...] + xs_buf[...] * dfull_ref[...]
    y = y * _silu(jnp.dot(hb, wz_ref[...], preferred_element_type=F32))
    gw = D_INNER // SSD_GROUPS
    parts = []
    for g in range(SSD_GROUPS):
        yg = y[:, g * gw:(g + 1) * gw]
        parts.append(yg * lax.rsqrt(jnp.mean(yg * yg, axis=-1, keepdims=True) + EPS))
    y = jnp.concatenate(parts, axis=-1) * nw_ref[...]
    out_b = _dot(y, wout_ref[...])
    gate = _sigmoid(jnp.dot(hb, wg_ref[...], preferred_element_type=F32) + gb_ref[...])
    mb_ref[0] = gate * out_b

    xbuf[0:SSD_HIST_ROWS, :] = xbuf[tt:tt + SSD_HIST_ROWS, :]

    @pl.when(t == nt - 1)
    def _():
        newssd_ref[0] = xbuf[0:SSD_HIST_ROWS, :]
        newh_ref[0] = state[...]


def _head_expand_matrix():
    e = np.zeros((LANES, D_INNER), np.float32)
    for h in range(SSD_HEADS):
        e[h, h * SSD_HEAD_DIM:(h + 1) * SSD_HEAD_DIM] = 1.0
    return jnp.asarray(e, BF16)


def _ssd_branch(x, st_ssd, st_h, npw, wz, wxbc, wdt, wg, gb, scw, scb, dtb, alog, dfull, nw, wout):
    b, t, _ = x.shape
    tt = _time_tile(t)
    nt = t // tt
    assert tt % SCAN_CHUNK == 0
    const = lambda i, j: (0, 0)
    wspec = lambda shape: pl.BlockSpec(shape, const, pipeline_mode=pl.Buffered(1))
    sst_pad = jnp.pad(st_ssd, ((0, 0), (SSD_HIST_ROWS - (SSD_CONV_WIDTH - 1), 0), (0, 0)))
    sh_t = jnp.transpose(st_h, (0, 3, 1, 2)).reshape(b, SSD_STATE, D_INNER)
    mb, newssd, newh = pl.pallas_call(
        functools.partial(_ssd_branch_kernel, tt=tt, nt=nt),
        out_shape=(jax.ShapeDtypeStruct((b, t, D_MODEL), F32),
                   jax.ShapeDtypeStruct((b, SSD_HIST_ROWS, SSD_XBC), F32),
                   jax.ShapeDtypeStruct((b, SSD_STATE, D_INNER), F32)),
        grid=(b, nt),
        in_specs=[pl.BlockSpec((1, tt, D_MODEL), lambda i, j: (i, j, 0)),
                  pl.BlockSpec((1, SSD_HIST_ROWS, SSD_XBC), lambda i, j: (i, 0, 0)),
                  pl.BlockSpec((1, SSD_STATE, D_INNER), lambda i, j: (i, 0, 0)),
                  wspec((1, D_MODEL)),
                  wspec((D_MODEL, D_INNER)),
                  wspec((D_MODEL, SSD_XBC)),
                  wspec((D_MODEL, LANES)),
                  wspec((D_MODEL, D_MODEL)),
                  wspec((1, D_MODEL)),
                  wspec((SSD_CONV_WIDTH, SSD_XBC)),
                  wspec((1, SSD_XBC)),
                  wspec((1, LANES)),
                  wspec((1, LANES)),
                  wspec((1, D_INNER)),
                  wspec((1, D_INNER)),
                  wspec((D_INNER, D_MODEL)),
                  wspec((LANES, D_INNER))],
        out_specs=(pl.BlockSpec((1, tt, D_MODEL), lambda i, j: (i, j, 0)),
                   pl.BlockSpec((1, SSD_HIST_ROWS, SSD_XBC), lambda i, j: (i, 0, 0)),
                   pl.BlockSpec((1, SSD_STATE, D_INNER), lambda i, j: (i, 0, 0))),
        scratch_shapes=[pltpu.VMEM((tt + SSD_HIST_ROWS, SSD_XBC), F32),
                        pltpu.VMEM((SSD_STATE, D_INNER), F32),
                        pltpu.VMEM((tt, D_INNER), F32),
                        pltpu.VMEM((tt, 2 * SSD_BC), F32),
                        pltpu.VMEM((tt, LANES), F32),
                        pltpu.VMEM((tt, LANES), F32),
                        pltpu.VMEM((tt, D_INNER), F32)],
        compiler_params=pltpu.CompilerParams(dimension_semantics=("arbitrary", "arbitrary"),
                                             vmem_limit_bytes=VMEM_LIMIT_BYTES),
        name="ssd_branch",
    )(x, sst_pad, sh_t, npw, wz, wxbc, wdt, wg, gb, scw, scb, dtb, alog, dfull, nw, wout, _head_expand_matrix())
    new_h = jnp.transpose(newh.reshape(b, SSD_STATE, SSD_HEADS, SSD_HEAD_DIM), (0, 2, 3, 1))
    return mb, newssd[:, SSD_HIST_ROWS - (SSD_CONV_WIDTH - 1):, :], new_h


def _attn_merge_kernel(x_ref, ma_ref, mb_ref, k_ref, v_ref, npw_ref, wq_ref, wxg_ref, wg_ref, gb_ref,
                       xow_ref, wo_ref, npost_ref, y_ref):
    x = x_ref[0]
    hb = _rms(x, npw_ref[...]).astype(BF16)
    q = jnp.dot(hb, wq_ref[...], preferred_element_type=F32)
    kb = k_ref[0].astype(BF16)
    vb = v_ref[0].astype(BF16)
    heads = []
    for hd in range(XA_HEADS):
        sl = slice(hd * XA_HEAD_DIM, (hd + 1) * XA_HEAD_DIM)
        s = _dot_nt(q[:, sl], kb[:, sl]) * (XA_HEAD_DIM ** -0.5)
        e = jnp.exp(s - jnp.max(s, axis=-1, keepdims=True))
        p = e / jnp.sum(e, axis=-1, keepdims=True)
        heads.append(_dot(p, vb[:, sl]))
    o = jnp.concatenate(heads, axis=-1)
    o = o * _silu(jnp.dot(hb, wxg_ref[...], preferred_element_type=F32))
    out_c = _dot(o, xow_ref[...])
    gate = _sigmoid(jnp.dot(hb, wg_ref[...], preferred_element_type=F32) + gb_ref[...])
    m = ma_ref[0] + mb_ref[0] + gate * out_c
    y_ref[0] = x + _rms(_dot(m, wo_ref[...]), npost_ref[...])


def _attn_merge(x, ma, mb, mem_k, mem_v, npw, wq, wxg, wg, gb, xow, wo, npost):
    b, t, _ = x.shape
    tt = _time_tile(t)
    nt = t // tt
    const = lambda i, j: (0, 0)
    wspec = lambda shape: pl.BlockSpec(shape, const, pipeline_mode=pl.Buffered(1))
    tile = pl.BlockSpec((1, tt, D_MODEL), lambda i, j: (i, j, 0))
    mem = pl.BlockSpec((1, N_MEM, XA_DIM), lambda i, j: (i, 0, 0))
    return pl.pallas_call(
        _attn_merge_kernel,
        out_shape=jax.ShapeDtypeStruct((b, t, D_MODEL), F32),
        grid=(b, nt),
        in_specs=[tile, tile, tile, mem, mem,
                  wspec((1, D_MODEL)),
                  wspec((D_MODEL, XA_DIM)),
                  wspec((D_MODEL, XA_DIM)),
                  wspec((D_MODEL, D_MODEL)),
                  wspec((1, D_MODEL)),
                  wspec((XA_DIM, D_MODEL)),
                  wspec((D_MODEL, D_MODEL)),
                  wspec((1, D_MODEL))],
        out_specs=tile,
        compiler_params=pltpu.CompilerParams(dimension_semantics=("arbitrary", "arbitrary"),
                                             vmem_limit_bytes=VMEM_LIMIT_BYTES),
        name="attn_merge",
    )(x, ma, mb, mem_k, mem_v, npw, wq, wxg, wg, gb, xow, wo, npost)


def _layer_weights(l, norm_pre_w, w_in, gate_b, conv_dw_w, conv_dw_b, conv_ln_w, conv_ln_b, conv_out_w,
                   ssd_conv_w, ssd_conv_b, ssd_dt_bias, ssd_a_log, ssd_d, ssd_norm_w, ssd_out_w,
                   xa_out_w, w_out, norm_post_w):
    offs = np.cumsum((0,) + IN_SIZES)
    seg = lambda i: w_in[l][:, offs[i]:offs[i + 1]].astype(BF16)
    row = lambda v: v.reshape(1, -1)
    pad_lanes = lambda v: jnp.pad(v, ((0, 0), (0, LANES - v.shape[-1])))
    gates_w = seg(8)
    gb = gate_b[l]
    return dict(
        npw=row(norm_pre_w[l]),
        w3=w_in[l][:, offs[0]:offs[3]].astype(BF16),
        wz=seg(3), wxbc=seg(4), wdt=pad_lanes(seg(5)), wq=seg(6), wxg=seg(7),
        wg=[gates_w[:, i * D_MODEL:(i + 1) * D_MODEL] for i in range(N_BRANCH)],
        gb=[row(gb[i * D_MODEL:(i + 1) * D_MODEL]) for i in range(N_BRANCH)],
        cw=conv_dw_w[l], cb=row(conv_dw_b[l]), lnw=row(conv_ln_w[l]), lnb=row(conv_ln_b[l]),
        cow=conv_out_w[l].astype(BF16),
        scw=ssd_conv_w[l], scb=row(ssd_conv_b[l]),
        dtb=pad_lanes(row(ssd_dt_bias[l])), alog=pad_lanes(row(ssd_a_log[l])),
        dfull=row(jnp.repeat(ssd_d[l], SSD_HEAD_DIM)), nw=row(ssd_norm_w[l]),
        wout=ssd_out_w[l].astype(BF16),
        xow=xa_out_w[l].astype(BF16), wo=w_out[l].astype(BF16), npost=row(norm_post_w[l]),
    )


def _layer(x, mem_k, mem_v, st_a, st_ssd, st_h, w):
    ma, new_a = _conv_branch(x, st_a, w["npw"], w["w3"], w["wg"][0], w["gb"][0], w["cw"], w["cb"],
                             w["lnw"], w["lnb"], w["cow"])
    mb, new_ssd, new_h = _ssd_branch(x, st_ssd, st_h, w["npw"], w["wz"], w["wxbc"], w["wdt"], w["wg"][1],
                                     w["gb"][1], w["scw"], w["scb"], w["dtb"], w["alog"], w["dfull"],
                                     w["nw"], w["wout"])
    y = _attn_merge(x, ma, mb, mem_k, mem_v, w["npw"], w["wq"], w["wxg"], w["wg"][2], w["gb"][2],
                    w["xow"], w["wo"], w["npost"])
    return y, new_a, new_ssd, new_h


def kernel(x_prompt, x_sample, mem_prompt, cache_mem_k, cache_mem_v, state_conv_a, state_conv_ssd, state_ssm,
           norm_pre_w, w_in, gate_b, conv_dw_w, conv_dw_b, conv_ln_w, conv_ln_b, conv_out_w,
           ssd_conv_w, ssd_conv_b, ssd_dt_bias, ssd_a_log, ssd_d, ssd_norm_w, ssd_out_w,
           mem_norm_w, xa_kv_w, xa_out_w, w_out, norm_post_w):
    bp = x_prompt.shape[0]
    bs = x_sample.shape[0]
    weights = [_layer_weights(l, norm_pre_w, w_in, gate_b, conv_dw_w, conv_dw_b, conv_ln_w, conv_ln_b,
                              conv_out_w, ssd_conv_w, ssd_conv_b, ssd_dt_bias, ssd_a_log, ssd_d, ssd_norm_w,
                              ssd_out_w, xa_out_w, w_out, norm_post_w) for l in range(DEPTH)]

    y_p = x_prompt
    mk_p, mv_p, ca_p, cs_p, hs_p = [], [], [], [], []
    for l in range(DEPTH):
        mk, mv = _mem_kv(mem_prompt, mem_norm_w[l], xa_kv_w[l])
        y_p, na, ns, nh = _layer(y_p, mk, mv,
                                 jnp.zeros((bp, CONV_WIDTH - 1, CONV_DIM), F32),
                                 jnp.zeros((bp, SSD_CONV_WIDTH - 1, SSD_XBC), F32),
                                 jnp.zeros((bp, SSD_HEADS, SSD_HEAD_DIM, SSD_STATE), F32),
                                 weights[l])
        mk_p.append(mk.reshape(bp, N_MEM, XA_HEADS, XA_HEAD_DIM))
        mv_p.append(mv.reshape(bp, N_MEM, XA_HEADS, XA_HEAD_DIM))
        ca_p.append(na); cs_p.append(ns); hs_p.append(nh)

    y_s = x_sample
    ca_s, cs_s, hs_s = [], [], []
    for l in range(DEPTH):
        y_s, na, ns, nh = _layer(y_s,
                                 cache_mem_k[l].reshape(bs, N_MEM, XA_DIM),
                                 cache_mem_v[l].reshape(bs, N_MEM, XA_DIM),
                                 state_conv_a[l], state_conv_ssd[l], state_ssm[l], weights[l])
        ca_s.append(na); cs_s.append(ns); hs_s.append(nh)

    return (y_p, y_s,
            jnp.stack(mk_p), jnp.stack(mv_p), jnp.stack(ca_p), jnp.stack(cs_p), jnp.stack(hs_p),
            jnp.stack(ca_s), jnp.stack(cs_s), jnp.stack(hs_s))
```

```python
import functools

import jax
import jax.numpy as jnp
import numpy as np
from jax import lax
from jax.experimental import pallas as pl
from jax.experimental.pallas import tpu as pltpu

D_MODEL = 1024
DEPTH = 2
N_MEM = 256
CONV_DIM = D_MODEL
CONV_WIDTH = 31
D_INNER = 2 * D_MODEL
SSD_HEAD_DIM = 64
SSD_HEADS = D_INNER // SSD_HEAD_DIM
SSD_GROUPS = 4
SSD_HPG = SSD_HEADS // SSD_GROUPS
SSD_STATE = 128
SSD_CONV_WIDTH = 4
SSD_BC = SSD_GROUPS * SSD_STATE
SSD_XBC = D_INNER + 2 * SSD_BC
XA_HEADS = 4
XA_HEAD_DIM = D_MODEL // XA_HEADS
XA_DIM = XA_HEADS * XA_HEAD_DIM
N_BRANCH = 3
IN_SIZES = (CONV_DIM, CONV_DIM, CONV_DIM, D_INNER, SSD_XBC, SSD_HEADS, XA_DIM, XA_DIM, N_BRANCH * D_MODEL)
EPS = 1e-6

LANES = 128
SUBLANES = 8
SCAN_CHUNK = 64
CONV_HIST_ROWS = 32
SSD_HIST_ROWS = 8
CONV_ROW_BLOCK = 64
PROJ_COL_BLOCK = 512
CONV_TAP_GROUP = 8
VMEM_LIMIT_BYTES =56 * 1024 * 1024

F32 = jnp.float32
BF16 = jnp.bfloat16


def _time_tile(t):
    return 256 if t % 256 == 0 else t


def _rms(x, w):
    return x * lax.rsqrt(jnp.mean(x * x, axis=-1, keepdims=True) + EPS) * w


def _sigmoid(x):
    return 0.5 + 0.5 * jnp.tanh(0.5 * x)


def _silu(x):
    hx = 0.5 * x
    return hx + hx * jnp.tanh(hx)


def _softplus(x):
    return jnp.maximum(x, 0.0) + jnp.log1p(jnp.exp(-jnp.abs(x)))


def _dot(a, b):
    return jnp.dot(a.astype(BF16), b.astype(BF16), preferred_element_type=F32)


def _dot_nt(a, b):
    return lax.dot_general(a.astype(BF16), b.astype(BF16), (((1,), (1,)), ((), ())),
                           preferred_element_type=F32)


def _dot_tn(a, b):
    return lax.dot_general(a.astype(BF16), b.astype(BF16), (((0,), (0,)), ((), ())),
                           preferred_element_type=F32)


def _split3(v):
    hi = v.astype(BF16)
    r1 = v - hi.astype(F32)
    mid = r1.astype(BF16)
    lo = (r1 - mid.astype(F32)).astype(BF16)
    return hi, mid, lo


def _dot_exact_rhs(a_bf16, v):
    hi, mid, lo = _split3(v)
    return (jnp.dot(a_bf16, hi, preferred_element_type=F32)
            + jnp.dot(a_bf16, mid, preferred_element_type=F32)
            + jnp.dot(a_bf16, lo, preferred_element_type=F32))


def _expand_heads(v, e_ref, parts):
    split = _split3(v)[:parts]
    return jnp.dot(jnp.concatenate(split, axis=-1), e_ref[0:parts * LANES, :], preferred_element_type=F32)


def _memkv_kernel(mem_ref, nw_ref, w_ref, k_ref, v_ref):
    h = _rms(mem_ref[0], nw_ref[0])
    kv = _dot(h, w_ref[0])
    k_ref[0, 0] = kv[:, :XA_DIM]
    v_ref[0, 0] = kv[:, XA_DIM:]


def _mem_kv(mem, norm_w, kv_w):
    b = mem.shape[0]
    layers = kv_w.shape[0]
    out = pl.BlockSpec((1, 1, N_MEM, XA_DIM), lambda l, i: (l, i, 0, 0))
    return pl.pallas_call(
        _memkv_kernel,
        out_shape=(jax.ShapeDtypeStruct((layers, b, N_MEM, XA_DIM), F32),
                   jax.ShapeDtypeStruct((layers, b, N_MEM, XA_DIM), F32)),
        grid=(layers, b),
        in_specs=[pl.BlockSpec((1, N_MEM, D_MODEL), lambda l, i: (i, 0, 0)),
                  pl.BlockSpec((1, 1, D_MODEL), lambda l, i: (l, 0, 0)),
                  pl.BlockSpec((1, D_MODEL, 2 * XA_DIM), lambda l, i: (l, 0, 0))],
        out_specs=(out, out),
        compiler_params=pltpu.CompilerParams(dimension_semantics=("arbitrary", "arbitrary"),
                                             vmem_limit_bytes=VMEM_LIMIT_BYTES),
        name="mem_kv",
    )(mem, norm_w.reshape(layers, 1, D_MODEL), kv_w.astype(BF16))


def _time_permutation(tt):
    n = tt // SUBLANES
    perm = np.zeros((tt, tt), np.float32)
    hist = np.zeros((tt, CONV_HIST_ROWS), np.float32)
    for rho in range(tt):
        i, s = divmod(rho, SUBLANES)
        perm[rho, s * n + i] = 1.0
    for m in range(CONV_HIST_ROWS):
        s = SUBLANES - CONV_HIST_ROWS // n + m // n
        hist[(m % n) * SUBLANES + s, m] = 1.0
    return perm, hist


def _conv_branch_kernel(x_ref, st_ref, perm_ref, permt_ref, hist_ref, histt_ref, npw_ref, w3_ref, wg_ref, gb_ref,
                        cw_ref, cb_ref, lnw_ref, lnb_ref, cow_ref, ma_ref, newa_ref, ext, cbuf, *, tt, nt, wraps):
    t = pl.program_id(1)
    n = tt // SUBLANES
    cur = wraps * tt

    nslab = CONV_DIM // LANES

    def put(ref, r0, val):
        for c in range(nslab):
            ref[c, r0:r0 + val.shape[0], :] = val[:, c * LANES:(c + 1) * LANES]

    def get(ref, r0, rows):
        return jnp.concatenate([ref[c, r0:r0 + rows, :] for c in range(nslab)], axis=-1)

    @pl.when(t == 0)
    def _():
        put(ext, cur, _dot_exact_rhs(hist_ref[...], st_ref[0]))

    hb = _rms(x_ref[0], npw_ref[...]).astype(BF16)
    hb_p = jnp.dot(perm_ref[...], hb, preferred_element_type=F32).astype(BF16)
    glu_v = jnp.dot(hb_p, w3_ref[:, 0:CONV_DIM], preferred_element_type=F32)
    glu_g = jnp.dot(hb_p, w3_ref[:, CONV_DIM:2 * CONV_DIM], preferred_element_type=F32)
    u = glu_v * _sigmoid(glu_g)

    prev3 = get(ext, cur, tt).reshape(n, SUBLANES, CONV_DIM)
    u3 = u.reshape(n, SUBLANES, CONV_DIM)
    sub = lax.broadcasted_iota(jnp.int32, (n, SUBLANES, CONV_DIM), 1)
    for q in range(1, wraps + 1):
        mixed = jnp.where(sub >= SUBLANES - q, prev3, u3)
        put(ext, (wraps - q) * tt, pltpu.roll(mixed, q, 1).reshape(tt, CONV_DIM))
    put(ext, cur, u)

    vregs_per_block = CONV_ROW_BLOCK // SUBLANES
    blocks_per_slab = tt // CONV_ROW_BLOCK

    def conv_block(idx, carry):
        c = idx // blocks_per_slab
        r0 = (idx % blocks_per_slab) * CONV_ROW_BLOCK
        first = cur + r0 - SUBLANES * (CONV_WIDTH - 1)
        acc = [jnp.broadcast_to(cb_ref[c], (SUBLANES, LANES))] * vregs_per_block
        for k0 in range(0, CONV_WIDTH, CONV_TAP_GROUP):
            taps = range(k0, min(k0 + CONV_TAP_GROUP, CONV_WIDTH))
            wk = [jnp.broadcast_to(cw_ref[c, k:k + 1, :], (SUBLANES, LANES)) for k in taps]
            win = [ext[c, pl.ds(pl.multiple_of(first + SUBLANES * (k0 + j), SUBLANES), SUBLANES), :]
                   for j in range(vregs_per_block + len(taps) - 1)]
            for m in range(vregs_per_block):
                for j in range(len(taps)):
                    acc[m] = acc[m] + wk[j] * win[m + j]
        for m in range(vregs_per_block):
            cbuf[c, pl.ds(pl.multiple_of(r0 + SUBLANES * m, SUBLANES), SUBLANES), :] = acc[m]
        return carry

    lax.fori_loop(0, nslab * blocks_per_slab, conv_block, 0)

    c = get(cbuf, 0, tt)
    mu = jnp.mean(c, axis=-1, keepdims=True)
    xc = c - mu
    y = xc * lax.rsqrt(jnp.mean(xc * xc, axis=-1, keepdims=True) + EPS) * lnw_ref[...] + lnb_ref[...]
    conv_gate = jnp.dot(hb_p, w3_ref[:, 2 * CONV_DIM:3 * CONV_DIM], preferred_element_type=F32)
    y = (_silu(y) * _silu(conv_gate)).astype(BF16)
    y = jnp.dot(permt_ref[...], y, preferred_element_type=F32).astype(BF16)
    out_a = jnp.dot(y, cow_ref[...], preferred_element_type=F32)
    gate = _sigmoid(jnp.dot(hb, wg_ref[...], preferred_element_type=F32) + gb_ref[...])
    ma_ref[0] = gate * out_a

    @pl.when(t == nt - 1)
    def _():
        newa_ref[0] = _dot_exact_rhs(histt_ref[...], get(ext, cur, tt))


def _conv_branch(x, st_a, npw, w3, wg, gb, cw, cb, lnw, lnb, cow):
    b, t, _ = x.shape
    tt = _time_tile(t)
    nt = t // tt
    n = tt // SUBLANES
    assert tt % SUBLANES == 0 and CONV_HIST_ROWS % n == 0 and CONV_HIST_ROWS // n <= SUBLANES
    wraps = -(-(CONV_WIDTH - 1) // n)
    assert wraps < SUBLANES and wraps * tt >= SUBLANES * (CONV_WIDTH - 1)
    perm, hist = _time_permutation(tt)
    nslab = CONV_DIM // LANES
    assert tt % CONV_ROW_BLOCK == 0
    const = lambda i, j: (0, 0)
    wspec = lambda shape: pl.BlockSpec(shape, const, pipeline_mode=pl.Buffered(1))
    st_pad = jnp.pad(st_a, ((0, 0), (CONV_HIST_ROWS - (CONV_WIDTH - 1), 0), (0, 0)))
    ma, newa = pl.pallas_call(
        functools.partial(_conv_branch_kernel, tt=tt, nt=nt, wraps=wraps),
        out_shape=(jax.ShapeDtypeStruct((b, t, D_MODEL), F32),
                   jax.ShapeDtypeStruct((b, CONV_HIST_ROWS, CONV_DIM), F32)),
        grid=(b, nt),
        in_specs=[pl.BlockSpec((1, tt, D_MODEL), lambda i, j: (i, j, 0)),
                  pl.BlockSpec((1, CONV_HIST_ROWS, CONV_DIM), lambda i, j: (i, 0, 0)),
                  wspec((tt, tt)),
                  wspec((tt, tt)),
                  wspec((tt, CONV_HIST_ROWS)),
                  wspec((CONV_HIST_ROWS, tt)),
                  wspec((1, D_MODEL)),
                  wspec((D_MODEL, 3 * CONV_DIM)),
                  wspec((D_MODEL, D_MODEL)),
                  wspec((1, D_MODEL)),
                  pl.BlockSpec((nslab, CONV_WIDTH, LANES), lambda i, j: (0, 0, 0), pipeline_mode=pl.Buffered(1)),
                  pl.BlockSpec((nslab, 1, LANES), lambda i, j: (0, 0, 0), pipeline_mode=pl.Buffered(1)),
                  wspec((1, CONV_DIM)),
                  wspec((1, CONV_DIM)),
                  wspec((CONV_DIM, D_MODEL))],
        out_specs=(pl.BlockSpec((1, tt, D_MODEL), lambda i, j: (i, j, 0)),
                   pl.BlockSpec((1, CONV_HIST_ROWS, CONV_DIM), lambda i, j: (i, 0, 0))),
        scratch_shapes=[pltpu.VMEM((nslab, (wraps + 1) * tt, LANES), F32),
                        pltpu.VMEM((nslab, tt, LANES), F32)],
        compiler_params=pltpu.CompilerParams(dimension_semantics=("arbitrary", "arbitrary"),
                                             vmem_limit_bytes=VMEM_LIMIT_BYTES),
        name="conv_branch",
    )(x, st_pad, jnp.asarray(perm, BF16), jnp.asarray(perm.T, BF16), jnp.asarray(hist, BF16),
      jnp.asarray(hist.T, BF16), npw, w3, wg, gb,
      jnp.transpose(cw.reshape(CONV_WIDTH, nslab, LANES), (1, 0, 2)), cb.reshape(nslab, 1, LANES),
      lnw, lnb, cow)
    return ma, newa[:, CONV_HIST_ROWS - (CONV_WIDTH - 1):, :]


def _ssd_branch_kernel(x_ref, sst_ref, sh_ref, npw_ref, wz_ref, wxbc_ref, wdt_ref, wg_ref, gb_ref,
                       scw_ref, scb_ref, dtb_ref, alog_ref, dfull_ref, nw_ref, wout_ref, e_ref, tril_ref,
                       mb_ref, newssd_ref, newh_ref,
                       xbuf, state, xs_buf, bc_buf, acsx_buf, xdt_buf, arow_buf, y_buf, zg_buf, gate_buf,
                       *, tt, nt):
    t = pl.program_id(1)
    L = SCAN_CHUNK

    @pl.when(t == 0)
    def _():
        xbuf[0:SSD_HIST_ROWS, :] = sst_ref[0]
        state[...] = jnp.transpose(sh_ref[0, 0].reshape(D_INNER, SSD_STATE))

    hb = _rms(x_ref[0], npw_ref[...]).astype(BF16)
    first_tap = SSD_HIST_ROWS - (SSD_CONV_WIDTH - 1)
    for c0 in range(0, SSD_XBC, PROJ_COL_BLOCK):
        cs = slice(c0, c0 + PROJ_COL_BLOCK)
        xbuf[SSD_HIST_ROWS:SSD_HIST_ROWS + tt, cs] = jnp.dot(hb, wxbc_ref[:, cs], preferred_element_type=F32)
        acc = jnp.broadcast_to(scb_ref[:, cs], (tt, PROJ_COL_BLOCK))
        for k in range(SSD_CONV_WIDTH):
            acc = acc + scw_ref[k:k + 1, cs] * xbuf[first_tap + k:first_tap + k + tt, cs]
        if c0 < D_INNER:
            xs_buf[:, cs] = _silu(acc)
        else:
            bc_buf[:, c0 - D_INNER:c0 - D_INNER + PROJ_COL_BLOCK] = _silu(acc)
    for c0 in range(0, D_INNER, PROJ_COL_BLOCK):
        cs = slice(c0, c0 + PROJ_COL_BLOCK)
        zg_buf[:, cs] = _silu(jnp.dot(hb, wz_ref[:, cs], preferred_element_type=F32))
    for c0 in range(0, D_MODEL, PROJ_COL_BLOCK):
        cs = slice(c0, c0 + PROJ_COL_BLOCK)
        gate_buf[:, cs] = _sigmoid(jnp.dot(hb, wg_ref[:, cs], preferred_element_type=F32) + gb_ref[:, cs])

    dt = _softplus(jnp.dot(hb, wdt_ref[...], preferred_element_type=F32) + dtb_ref[...])
    acs = _dot_exact_rhs(tril_ref[...], dt * (-jnp.exp(alog_ref[...])))
    acsx_buf[...] = _expand_heads(acs, e_ref, 3)
    xdt_buf[...] = xs_buf[...] * _expand_heads(dt, e_ref, 2)

    acs_t = jnp.transpose(acs)
    for ci in range(tt // L):
        pieces = []
        for j in range(SSD_HEADS // 2):
            pieces.append(jnp.concatenate([acs_t[2 * j:2 * j + 1, ci * L:(ci + 1) * L],
                                           acs_t[2 * j + 1:2 * j + 2, ci * L:(ci + 1) * L]], axis=-1))
        arow_buf[ci:ci + 1, :] = jnp.concatenate(pieces, axis=-1)

    quad = 4 * SSD_HEAD_DIM
    lane = lax.broadcasted_iota(jnp.int32, (L, quad), 1)
    causal = lax.broadcasted_iota(jnp.int32, (L, quad), 0) >= (lane & (L - 1))
    head_of_lane = lane // SSD_HEAD_DIM

    def chunk_body(ci, carry):
        r0 = pl.multiple_of(ci * L, L)
        rows = pl.ds(r0, L)
        ax = acsx_buf[rows, :]
        arow = arow_buf[pl.ds(ci, 1), :]
        total = acsx_buf[pl.ds(r0 + L - 1, 1), :]
        xdt = xdt_buf[rows, :]
        xw = xdt * jnp.exp(total - ax)
        dfs = jnp.exp(ax)
        cd = jnp.exp(total)
        for g in range(SSD_GROUPS):
            gs0 = g * SSD_HPG * SSD_HEAD_DIM
            gs = slice(gs0, gs0 + SSD_HPG * SSD_HEAD_DIM)
            bg = bc_buf[rows, g * SSD_STATE:(g + 1) * SSD_STATE].astype(BF16)
            cg = bc_buf[rows, SSD_BC + g * SSD_STATE:SSD_BC + (g + 1) * SSD_STATE].astype(BF16)
            cb4 = _dot_nt(cg, jnp.concatenate([bg] * 4, axis=0))
            sg = state[:, gs]
            y_off = _dot(cg, sg) * dfs[:, gs]
            for q in range(SSD_HPG // 4):
                qs = slice(gs0 + q * quad, gs0 + (q + 1) * quad)
                gm = jnp.where(causal, jnp.exp(ax[:, qs] - arow[:, qs]), 0.0) * cb4
                xq = xdt[:, qs].astype(BF16)
                blockdiag = jnp.concatenate(
                    [jnp.where(head_of_lane == i, xq, jnp.zeros_like(xq)) for i in range(4)], axis=0)
                y_buf[rows, qs] = (jnp.dot(gm.astype(BF16), blockdiag, preferred_element_type=F32)
                                   + y_off[:, q * quad:(q + 1) * quad])
            state[:, gs] = sg * cd[:, gs] + _dot_tn(bg, xw[:, gs])
        return carry

    lax.fori_loop(0, tt // L, chunk_body, 0, unroll=2 if (tt // L) % 2 == 0 else 1)

    y = y_buf[...] + xs_buf[...] * dfull_ref[...]
    y = y * zg_buf[...]
    gw = D_INNER // SSD_GROUPS
    parts = []
    for g in range(SSD_GROUPS):
        yg = y[:, g * gw:(g + 1) * gw]
        parts.append(yg * lax.rsqrt(jnp.mean(yg * yg, axis=-1, keepdims=True) + EPS))
    y = jnp.concatenate(parts, axis=-1) * nw_ref[...]
    out_b = _dot(y, wout_ref[...])
    mb_ref[0] = gate_buf[...] * out_b

    xbuf[0:SSD_HIST_ROWS, :] = xbuf[tt:tt + SSD_HIST_ROWS, :]

    @pl.when(t == nt - 1)
    def _():
        newssd_ref[0] = xbuf[0:SSD_HIST_ROWS, :]
        newh_ref[0] = jnp.transpose(state[...]).reshape(SSD_HEADS, SSD_HEAD_DIM, SSD_STATE)


def _head_expand_matrix():
    e = np.zeros((LANES, D_INNER), np.float32)
    for h in range(SSD_HEADS):
        e[h, h * SSD_HEAD_DIM:(h + 1) * SSD_HEAD_DIM] = 1.0
    return jnp.asarray(np.concatenate([e, e, e], axis=0), BF16)


def _chunk_cumsum_matrix(tt):
    r = np.arange(tt)
    same_chunk = (r[:, None] // SCAN_CHUNK) == (r[None, :] // SCAN_CHUNK)
    return jnp.asarray((same_chunk & (r[:, None] >= r[None, :])).astype(np.float32), BF16)


def _ssd_branch(x, st_ssd, st_h, layer, npw, wz, wxbc, wdt, wg, gb, scw, scb, dtb, alog, dfull, nw, wout):
    b, t, _ = x.shape
    tt = _time_tile(t)
    nt = t // tt
    assert tt % SCAN_CHUNK == 0 and tt // SCAN_CHUNK <= SUBLANES
    const = lambda i, j: (0, 0)
    wspec = lambda shape: pl.BlockSpec(shape, const, pipeline_mode=pl.Buffered(1))
    sst_pad = jnp.pad(st_ssd, ((0, 0), (SSD_HIST_ROWS - (SSD_CONV_WIDTH - 1), 0), (0, 0)))
    mb, newssd, newh = pl.pallas_call(
        functools.partial(_ssd_branch_kernel, tt=tt, nt=nt),
        out_shape=(jax.ShapeDtypeStruct((b, t, D_MODEL), F32),
                   jax.ShapeDtypeStruct((b, SSD_HIST_ROWS, SSD_XBC), F32),
                   jax.ShapeDtypeStruct((b, SSD_HEADS, SSD_HEAD_DIM, SSD_STATE), F32)),
        grid=(b, nt),
        in_specs=[pl.BlockSpec((1, tt, D_MODEL), lambda i, j: (i, j, 0)),
                  pl.BlockSpec((1, SSD_HIST_ROWS, SSD_XBC), lambda i, j: (i, 0, 0)),
                  pl.BlockSpec((1, 1, SSD_HEADS, SSD_HEAD_DIM, SSD_STATE), lambda i, j: (layer, i, 0, 0, 0)),
                  wspec((1, D_MODEL)),
                  wspec((D_MODEL, D_INNER)),
                  wspec((D_MODEL, SSD_XBC)),
                  wspec((D_MODEL, LANES)),
                  wspec((D_MODEL, D_MODEL)),
                  wspec((1, D_MODEL)),
                  wspec((SSD_CONV_WIDTH, SSD_XBC)),
                  wspec((1, SSD_XBC)),
                  wspec((1, LANES)),
                  wspec((1, LANES)),
                  wspec((1, D_INNER)),
                  wspec((1, D_INNER)),
                  wspec((D_INNER, D_MODEL)),
                  wspec((3 * LANES, D_INNER)),
                  wspec((tt, tt))],
        out_specs=(pl.BlockSpec((1, tt, D_MODEL), lambda i, j: (i, j, 0)),
                   pl.BlockSpec((1, SSD_HIST_ROWS, SSD_XBC), lambda i, j: (i, 0, 0)),
                   pl.BlockSpec((1, SSD_HEADS, SSD_HEAD_DIM, SSD_STATE), lambda i, j: (i, 0, 0, 0))),
        scratch_shapes=[pltpu.VMEM((tt + SSD_HIST_ROWS, SSD_XBC), F32),
                        pltpu.VMEM((SSD_STATE, D_INNER), F32),
                        pltpu.VMEM((tt, D_INNER), F32),
                        pltpu.VMEM((tt, 2 * SSD_BC), F32),
                        pltpu.VMEM((tt, D_INNER), F32),
                        pltpu.VMEM((tt, D_INNER), F32),
                        pltpu.VMEM((SUBLANES, D_INNER), F32),
                        pltpu.VMEM((tt, D_INNER), F32),
                        pltpu.VMEM((tt, D_INNER), F32),
                        pltpu.VMEM((tt, D_MODEL), F32)],
        compiler_params=pltpu.CompilerParams(dimension_semantics=("arbitrary", "arbitrary"),
                                             vmem_limit_bytes=VMEM_LIMIT_BYTES),
        name="ssd_branch",
    )(x, sst_pad, st_h, npw, wz, wxbc, wdt, wg, gb, scw, scb, dtb, alog, dfull, nw, wout, _head_expand_matrix(),
      _chunk_cumsum_matrix(tt))
    return mb, newssd[:, SSD_HIST_ROWS - (SSD_CONV_WIDTH - 1):, :], newh


def _attn_merge_kernel(x_ref, ma_ref, mb_ref, k_ref, v_ref, npw_ref, wq_ref, wxg_ref, wg_ref, gb_ref,
                       xow_ref, wo_ref, npost_ref, y_ref):
    x = x_ref[0]
    hb = _rms(x, npw_ref[...]).astype(BF16)
    q = jnp.dot(hb, wq_ref[...], preferred_element_type=F32)
    kb = k_ref[0, 0].astype(BF16)
    vb = v_ref[0, 0].astype(BF16)
    heads = []
    for hd in range(XA_HEADS):
        sl = slice(hd * XA_HEAD_DIM, (hd + 1) * XA_HEAD_DIM)
        s = _dot_nt(q[:, sl], kb[:, sl]) * (XA_HEAD_DIM ** -0.5)
        e = jnp.exp(s - jnp.max(s, axis=-1, keepdims=True))
        p = e / jnp.sum(e, axis=-1, keepdims=True)
        heads.append(_dot(p, vb[:, sl]))
    o = jnp.concatenate(heads, axis=-1)
    o = o * _silu(jnp.dot(hb, wxg_ref[...], preferred_element_type=F32))
    out_c = _dot(o, xow_ref[...])
    gate = _sigmoid(jnp.dot(hb, wg_ref[...], preferred_element_type=F32) + gb_ref[...])
    m = ma_ref[0] + mb_ref[0] + gate * out_c
    y_ref[0] = x + _rms(_dot(m, wo_ref[...]), npost_ref[...])


def _attn_merge(x, ma, mb, mem_k, mem_v, layer, npw, wq, wxg, wg, gb, xow, wo, npost):
    b, t, _ = x.shape
    tt = _time_tile(t)
    nt = t // tt
    const = lambda i, j: (0, 0)
    wspec = lambda shape: pl.BlockSpec(shape, const, pipeline_mode=pl.Buffered(1))
    tile = pl.BlockSpec((1, tt, D_MODEL), lambda i, j: (i, j, 0))
    mem = pl.BlockSpec((1, 1, N_MEM, XA_DIM), lambda i, j: (layer, i, 0, 0))
    return pl.pallas_call(
        _attn_merge_kernel,
        out_shape=jax.ShapeDtypeStruct((b, t, D_MODEL), F32),
        grid=(b, nt),
        in_specs=[tile, tile, tile, mem, mem,
                  wspec((1, D_MODEL)),
                  wspec((D_MODEL, XA_DIM)),
                  wspec((D_MODEL, XA_DIM)),
                  wspec((D_MODEL, D_MODEL)),
                  wspec((1, D_MODEL)),
                  wspec((XA_DIM, D_MODEL)),
                  wspec((D_MODEL, D_MODEL)),
                  wspec((1, D_MODEL))],
        out_specs=tile,
        compiler_params=pltpu.CompilerParams(dimension_semantics=("arbitrary", "arbitrary"),
                                             vmem_limit_bytes=VMEM_LIMIT_BYTES),
        name="attn_merge",
    )(x, ma, mb, mem_k, mem_v, npw, wq, wxg, wg, gb, xow, wo, npost)


def _layer_weights(l, norm_pre_w, w_in, gate_b, conv_dw_w, conv_dw_b, conv_ln_w, conv_ln_b, conv_out_w,
                   ssd_conv_w, ssd_conv_b, ssd_dt_bias, ssd_a_log, ssd_d, ssd_norm_w, ssd_out_w,
                   xa_out_w, w_out, norm_post_w):
    offs = np.cumsum((0,) + IN_SIZES)
    seg = lambda i: w_in[l][:, offs[i]:offs[i + 1]].astype(BF16)
    row = lambda v: v.reshape(1, -1)
    pad_lanes = lambda v: jnp.pad(v, ((0, 0), (0, LANES - v.shape[-1])))
    gates_w = seg(8)
    gb = gate_b[l]
    return dict(
        npw=row(norm_pre_w[l]),
        w3=w_in[l][:, offs[0]:offs[3]].astype(BF16),
        wz=seg(3), wxbc=seg(4), wdt=pad_lanes(seg(5)), wq=seg(6), wxg=seg(7),
        wg=[gates_w[:, i * D_MODEL:(i + 1) * D_MODEL] for i in range(N_BRANCH)],
        gb=[row(gb[i * D_MODEL:(i + 1) * D_MODEL]) for i in range(N_BRANCH)],
        cw=conv_dw_w[l], cb=row(conv_dw_b[l]), lnw=row(conv_ln_w[l]), lnb=row(conv_ln_b[l]),
        cow=conv_out_w[l].astype(BF16),
        scw=ssd_conv_w[l], scb=row(ssd_conv_b[l]),
        dtb=pad_lanes(row(ssd_dt_bias[l])), alog=pad_lanes(row(ssd_a_log[l])),
        dfull=row(jnp.repeat(ssd_d[l], SSD_HEAD_DIM)), nw=row(ssd_norm_w[l]),
        wout=ssd_out_w[l].astype(BF16),
        xow=xa_out_w[l].astype(BF16), wo=w_out[l].astype(BF16), npost=row(norm_post_w[l]),
    )


def _layer(x, mem_k, mem_v, mem_layer, st_a, st_ssd, st_h, state_layer, w):
    ma, new_a = _conv_branch(x, st_a, w["npw"], w["w3"], w["wg"][0], w["gb"][0], w["cw"], w["cb"],
                             w["lnw"], w["lnb"], w["cow"])
    mb, new_ssd, new_h = _ssd_branch(x, st_ssd, st_h, state_layer, w["npw"], w["wz"], w["wxbc"], w["wdt"], w["wg"][1],
                                     w["gb"][1], w["scw"], w["scb"], w["dtb"], w["alog"], w["dfull"],
                                     w["nw"], w["wout"])
    y = _attn_merge(x, ma, mb, mem_k, mem_v, mem_layer, w["npw"], w["wq"], w["wxg"], w["wg"][2], w["gb"][2],
                    w["xow"], w["wo"], w["npost"])
    return y, new_a, new_ssd, new_h


def kernel(x_prompt, x_sample, mem_prompt, cache_mem_k, cache_mem_v, state_conv_a, state_conv_ssd, state_ssm,
           norm_pre_w, w_in, gate_b, conv_dw_w, conv_dw_b, conv_ln_w, conv_ln_b, conv_out_w,
           ssd_conv_w, ssd_conv_b, ssd_dt_bias, ssd_a_log, ssd_d, ssd_norm_w, ssd_out_w,
           mem_norm_w, xa_kv_w, xa_out_w, w_out, norm_post_w):
    bp = x_prompt.shape[0]
    bs = x_sample.shape[0]
    weights = [_layer_weights(l, norm_pre_w, w_in, gate_b, conv_dw_w, conv_dw_b, conv_ln_w, conv_ln_b,
                              conv_out_w, ssd_conv_w, ssd_conv_b, ssd_dt_bias, ssd_a_log, ssd_d, ssd_norm_w,
                              ssd_out_w, xa_out_w, w_out, norm_post_w) for l in range(DEPTH)]

    y_p = x_prompt
    mk_p, mv_p = _mem_kv(mem_prompt, mem_norm_w, xa_kv_w)
    zero_h = jnp.zeros((1, bp, SSD_HEADS, SSD_HEAD_DIM, SSD_STATE), F32)
    ca_p, cs_p, hs_p = [], [], []
    for l in range(DEPTH):
        y_p, na, ns, nh = _layer(y_p, mk_p, mv_p, l,
                                 jnp.zeros((bp, CONV_WIDTH - 1, CONV_DIM), F32),
                                 jnp.zeros((bp, SSD_CONV_WIDTH - 1, SSD_XBC), F32),
                                 zero_h, 0, weights[l])
        ca_p.append(na); cs_p.append(ns); hs_p.append(nh)

    y_s = x_sample
    mk_s = cache_mem_k.reshape(DEPTH, bs, N_MEM, XA_DIM)
    mv_s = cache_mem_v.reshape(DEPTH, bs, N_MEM, XA_DIM)
    ca_s, cs_s, hs_s = [], [], []
    for l in range(DEPTH):
        y_s, na, ns, nh = _layer(y_s, mk_s, mv_s, l, state_conv_a[l], state_conv_ssd[l], state_ssm, l, weights[l])
        ca_s.append(na); cs_s.append(ns); hs_s.append(nh)

    return (y_p, y_s,
            mk_p.reshape(DEPTH, bp, N_MEM, XA_HEADS, XA_HEAD_DIM),
            mv_p.reshape(DEPTH, bp, N_MEM, XA_HEADS, XA_HEAD_DIM),
            jnp.stack(ca_p), jnp.stack(cs_p), jnp.stack(hs_p),
            jnp.stack(ca_s), jnp.stack(cs_s), jnp.stack(hs_s))
```

```python
import functools

import jax
import jax.numpy as jnp
import numpy as np
from jax import lax
from jax.experimental import pallas as pl
from jax.experimental.pallas import tpu as pltpu

D_MODEL = 1024
DEPTH = 2
N_MEM = 256
CONV_DIM = D_MODEL
CONV_WIDTH = 31
D_INNER = 2 * D_MODEL
SSD_HEAD_DIM = 64
SSD_HEADS = D_INNER // SSD_HEAD_DIM
SSD_GROUPS = 4
SSD_HPG = SSD_HEADS // SSD_GROUPS
SSD_STATE = 128
SSD_CONV_WIDTH = 4
SSD_BC = SSD_GROUPS * SSD_STATE
SSD_XBC = D_INNER + 2 * SSD_BC
XA_HEADS = 4
XA_HEAD_DIM = D_MODEL // XA_HEADS
XA_DIM = XA_HEADS * XA_HEAD_DIM
N_BRANCH = 3
IN_SIZES = (CONV_DIM, CONV_DIM, CONV_DIM, D_INNER, SSD_XBC, SSD_HEADS, XA_DIM, XA_DIM, N_BRANCH * D_MODEL)
EPS = 1e-6

LANES = 128
SUBLANES = 8
SCAN_CHUNK = 64
CONV_HIST_ROWS = 32
SSD_HIST_ROWS = 8
CONV_ROW_BLOCK = 64
MXU_ROWS_TARGET = 256
SSD_MAX_ROWS_PER_STEP = 2
PROJ_COL_BLOCK = 512
CONV_TAP_GROUP = 8
VMEM_LIMIT_BYTES = 56 * 1024 * 1024

F32 = jnp.float32
BF16 = jnp.bfloat16


def _time_tile(t):
    return 256 if t % 256 == 0 else t


def _rows_per_step(b, tt, most=None):
    bb = max(1, MXU_ROWS_TARGET // tt)
    if most is not None:
        bb = min(bb, most)
    while b % bb:
        bb -= 1
    return bb


def _rms(x, w):
    return x * lax.rsqrt(jnp.mean(x * x, axis=-1, keepdims=True) + EPS) * w


def _sigmoid(x):
    return 0.5 + 0.5 * jnp.tanh(0.5 * x)


def _silu(x):
    hx = 0.5 * x
    return hx + hx * jnp.tanh(hx)


def _softplus(x):
    return jnp.maximum(x, 0.0) + jnp.log1p(jnp.exp(-jnp.abs(x)))


def _dot(a, b):
    return jnp.dot(a.astype(BF16), b.astype(BF16), preferred_element_type=F32)


def _dot_nt(a, b):
    return lax.dot_general(a.astype(BF16), b.astype(BF16), (((1,), (1,)), ((), ())),
                           preferred_element_type=F32)


def _dot_tn(a, b):
    return lax.dot_general(a.astype(BF16), b.astype(BF16), (((0,), (0,)), ((), ())),
                           preferred_element_type=F32)


def _split3(v):
    hi = v.astype(BF16)
    r1 = v - hi.astype(F32)
    mid = r1.astype(BF16)
    lo = (r1 - mid.astype(F32)).astype(BF16)
    return hi, mid, lo


def _dot_exact_rhs(a_bf16, v):
    hi, mid, lo = _split3(v)
    return (jnp.dot(a_bf16, hi, preferred_element_type=F32)
            + jnp.dot(a_bf16, mid, preferred_element_type=F32)
            + jnp.dot(a_bf16, lo, preferred_element_type=F32))


def _expand_heads(v, e_ref, parts):
    split = _split3(v)[:parts]
    return jnp.dot(jnp.concatenate(split, axis=-1), e_ref[0:parts * LANES, :], preferred_element_type=F32)


def _memkv_kernel(mem_ref, nw_ref, w_ref, k_ref, v_ref):
    h = _rms(mem_ref[0], nw_ref[0])
    kv = _dot(h, w_ref[0])
    k_ref[0, 0] = kv[:, :XA_DIM]
    v_ref[0, 0] = kv[:, XA_DIM:]


def _mem_kv(mem, norm_w, kv_w):
    b = mem.shape[0]
    layers = kv_w.shape[0]
    out = pl.BlockSpec((1, 1, N_MEM, XA_DIM), lambda l, i: (l, i, 0, 0))
    return pl.pallas_call(
        _memkv_kernel,
        out_shape=(jax.ShapeDtypeStruct((layers, b, N_MEM, XA_DIM), F32),
                   jax.ShapeDtypeStruct((layers, b, N_MEM, XA_DIM), F32)),
        grid=(layers, b),
        in_specs=[pl.BlockSpec((1, N_MEM, D_MODEL), lambda l, i: (i, 0, 0)),
                  pl.BlockSpec((1, 1, D_MODEL), lambda l, i: (l, 0, 0)),
                  pl.BlockSpec((1, D_MODEL, 2 * XA_DIM), lambda l, i: (l, 0, 0))],
        out_specs=(out, out),
        compiler_params=pltpu.CompilerParams(dimension_semantics=("arbitrary", "arbitrary"),
                                             vmem_limit_bytes=VMEM_LIMIT_BYTES),
        name="mem_kv",
    )(mem, norm_w.reshape(layers, 1, D_MODEL), kv_w.astype(BF16))


def _time_permutation(tt):
    n = tt // SUBLANES
    perm = np.zeros((tt, tt), np.float32)
    hist = np.zeros((tt, CONV_HIST_ROWS), np.float32)
    for rho in range(tt):
        i, s = divmod(rho, SUBLANES)
        perm[rho, s * n + i] = 1.0
    for m in range(CONV_HIST_ROWS):
        s = SUBLANES - CONV_HIST_ROWS // n + m // n
        hist[(m % n) * SUBLANES + s, m] = 1.0
    return perm, hist


def _conv_branch_kernel(x_ref, st_ref, perm_ref, permt_ref, hist_ref, histt_ref, npw_ref, w3_ref, wg_ref, gb_ref,
                        cw_ref, cb_ref, lnw_ref, lnb_ref, cow_ref, ma_ref, newa_ref, ext, cbuf,
                        *, bb, tt, nt, wraps):
    t = pl.program_id(1)
    n = tt // SUBLANES
    rows = bb * tt
    span = (wraps + 1) * tt
    nslab = CONV_DIM // LANES

    def put(ref, r0, val):
        for c in range(nslab):
            ref[c, r0:r0 + val.shape[0], :] = val[:, c * LANES:(c + 1) * LANES]

    def get(ref, r0, nrows):
        return jnp.concatenate([ref[c, r0:r0 + nrows, :] for c in range(nslab)], axis=-1)

    @pl.when(t == 0)
    def _():
        for r in range(bb):
            put(ext, r * span + wraps * tt, _dot_exact_rhs(hist_ref[...], st_ref[r]))

    hb = _rms(x_ref[...].reshape(rows, D_MODEL), npw_ref[...]).astype(BF16)
    hb_p = jnp.dot(perm_ref[...], hb, preferred_element_type=F32).astype(BF16)
    glu_v = jnp.dot(hb_p, w3_ref[:, 0:CONV_DIM], preferred_element_type=F32)
    glu_g = jnp.dot(hb_p, w3_ref[:, CONV_DIM:2 * CONV_DIM], preferred_element_type=F32)
    u = glu_v * _sigmoid(glu_g)

    sub = lax.broadcasted_iota(jnp.int32, (n, SUBLANES, CONV_DIM), 1)
    for r in range(bb):
        cur = r * span + wraps * tt
        prev3 = get(ext, cur, tt).reshape(n, SUBLANES, CONV_DIM)
        u3 = u[r * tt:(r + 1) * tt].reshape(n, SUBLANES, CONV_DIM)
        for q in range(1, wraps + 1):
            mixed = jnp.where(sub >= SUBLANES - q, prev3, u3)
            put(ext, r * span + (wraps - q) * tt, pltpu.roll(mixed, q, 1).reshape(tt, CONV_DIM))
        put(ext, cur, u[r * tt:(r + 1) * tt])

    vregs_per_block = CONV_ROW_BLOCK // SUBLANES
    blocks_per_row = tt // CONV_ROW_BLOCK
    blocks_per_slab = bb * blocks_per_row

    def conv_block(idx, carry):
        c = idx // blocks_per_slab
        blk = idx % blocks_per_slab
        r = blk // blocks_per_row
        r0 = (blk % blocks_per_row) * CONV_ROW_BLOCK
        first = r * span + wraps * tt + r0 - SUBLANES * (CONV_WIDTH - 1)
        acc = [jnp.broadcast_to(cb_ref[c], (SUBLANES, LANES))] * vregs_per_block
        for k0 in range(0, CONV_WIDTH, CONV_TAP_GROUP):
            taps = range(k0, min(k0 + CONV_TAP_GROUP, CONV_WIDTH))
            wk = [jnp.broadcast_to(cw_ref[c, k:k + 1, :], (SUBLANES, LANES)) for k in taps]
            win = [ext[c, pl.ds(pl.multiple_of(first + SUBLANES * (k0 + j), SUBLANES), SUBLANES), :]
                   for j in range(vregs_per_block + len(taps) - 1)]
            for m in range(vregs_per_block):
                for j in range(len(taps)):
                    acc[m] = acc[m] + wk[j] * win[m + j]
        for m in range(vregs_per_block):
            cbuf[c, pl.ds(pl.multiple_of(r * tt + r0 + SUBLANES * m, SUBLANES), SUBLANES), :] = acc[m]
        return carry

    lax.fori_loop(0, nslab * blocks_per_slab, conv_block, 0)

    c = get(cbuf, 0, rows)
    mu = jnp.mean(c, axis=-1, keepdims=True)
    xc = c - mu
    y = xc * lax.rsqrt(jnp.mean(xc * xc, axis=-1, keepdims=True) + EPS) * lnw_ref[...] + lnb_ref[...]
    conv_gate = jnp.dot(hb_p, w3_ref[:, 2 * CONV_DIM:3 * CONV_DIM], preferred_element_type=F32)
    y = (_silu(y) * _silu(conv_gate)).astype(BF16)
    y = jnp.dot(permt_ref[...], y, preferred_element_type=F32).astype(BF16)
    out_a = jnp.dot(y, cow_ref[...], preferred_element_type=F32)
    gate = _sigmoid(jnp.dot(hb, wg_ref[...], preferred_element_type=F32) + gb_ref[...])
    ma_ref[...] = (gate * out_a).reshape(bb, tt, D_MODEL)

    @pl.when(t == nt - 1)
    def _():
        for r in range(bb):
            newa_ref[r] = _dot_exact_rhs(histt_ref[...], get(ext, r * span + wraps * tt, tt))


def _conv_branch(x, st_a, npw, w3, wg, gb, cw, cb, lnw, lnb, cow):
    b, t, _ = x.shape
    tt = _time_tile(t)
    bb = _rows_per_step(b, tt)
    nt = t // tt
    n = tt // SUBLANES
    rows = bb * tt
    assert tt % SUBLANES == 0 and CONV_HIST_ROWS % n == 0 and CONV_HIST_ROWS // n <= SUBLANES
    wraps = -(-(CONV_WIDTH - 1) // n)
    assert wraps < SUBLANES and wraps * tt >= SUBLANES * (CONV_WIDTH - 1)
    perm, hist = _time_permutation(tt)
    perm = np.kron(np.eye(bb, dtype=np.float32), perm)
    nslab = CONV_DIM // LANES
    assert tt % CONV_ROW_BLOCK == 0
    const = lambda i, j: (0, 0)
    wspec = lambda shape: pl.BlockSpec(shape, const, pipeline_mode=pl.Buffered(1))
    st_pad = jnp.pad(st_a, ((0, 0), (CONV_HIST_ROWS - (CONV_WIDTH - 1), 0), (0, 0)))
    ma, newa = pl.pallas_call(
        functools.partial(_conv_branch_kernel, bb=bb, tt=tt, nt=nt, wraps=wraps),
        out_shape=(jax.ShapeDtypeStruct((b, t, D_MODEL), F32),
                   jax.ShapeDtypeStruct((b, CONV_HIST_ROWS, CONV_DIM), F32)),
        grid=(b // bb, nt),
        in_specs=[pl.BlockSpec((bb, tt, D_MODEL), lambda i, j: (i, j, 0)),
                  pl.BlockSpec((bb, CONV_HIST_ROWS, CONV_DIM), lambda i, j: (i, 0, 0)),
                  wspec((rows, rows)),
                  wspec((rows, rows)),
                  wspec((tt, CONV_HIST_ROWS)),
                  wspec((CONV_HIST_ROWS, tt)),
                  wspec((1, D_MODEL)),
                  wspec((D_MODEL, 3 * CONV_DIM)),
                  wspec((D_MODEL, D_MODEL)),
                  wspec((1, D_MODEL)),
                  pl.BlockSpec((nslab, CONV_WIDTH, LANES), lambda i, j: (0, 0, 0), pipeline_mode=pl.Buffered(1)),
                  pl.BlockSpec((nslab, 1, LANES), lambda i, j: (0, 0, 0), pipeline_mode=pl.Buffered(1)),
                  wspec((1, CONV_DIM)),
                  wspec((1, CONV_DIM)),
                  wspec((CONV_DIM, D_MODEL))],
        out_specs=(pl.BlockSpec((bb, tt, D_MODEL), lambda i, j: (i, j, 0)),
                   pl.BlockSpec((bb, CONV_HIST_ROWS, CONV_DIM), lambda i, j: (i, 0, 0))),
        scratch_shapes=[pltpu.VMEM((nslab, bb * (wraps + 1) * tt, LANES), F32),
                        pltpu.VMEM((nslab, rows, LANES), F32)],
        compiler_params=pltpu.CompilerParams(dimension_semantics=("arbitrary", "arbitrary"),
                                             vmem_limit_bytes=VMEM_LIMIT_BYTES),
        name="conv_branch",
    )(x, st_pad, jnp.asarray(perm, BF16), jnp.asarray(perm.T, BF16), jnp.asarray(hist, BF16),
      jnp.asarray(hist.T, BF16), npw, w3, wg, gb,
      jnp.transpose(cw.reshape(CONV_WIDTH, nslab, LANES), (1, 0, 2)), cb.reshape(nslab, 1, LANES),
      lnw, lnb, cow)
    return ma, newa[:, CONV_HIST_ROWS - (CONV_WIDTH - 1):, :]


def _ssd_branch_kernel(x_ref, sst_ref, sh_ref, npw_ref, wz_ref, wxbc_ref, wdt_ref, wg_ref, gb_ref,
                       scw_ref, scb_ref, dtb_ref, alog_ref, dfull_ref, nw_ref, wout_ref, e_ref, tril_ref,
                       mb_ref, newssd_ref, newh_ref,
                       xbuf, state, xs_buf, bc_buf, acsx_buf, xdt_buf, arow_buf, y_buf, zg_buf, gate_buf,
                       *, bb, tt, nt):
    t = pl.program_id(1)
    L = SCAN_CHUNK
    rows = bb * tt
    chunks_per_row = tt // L

    @pl.when(t == 0)
    def _():
        for r in range(bb):
            xbuf[r, 0:SSD_HIST_ROWS, :] = sst_ref[r]
            state[r] = jnp.transpose(sh_ref[0, r].reshape(D_INNER, SSD_STATE))

    hb = _rms(x_ref[...].reshape(rows, D_MODEL), npw_ref[...]).astype(BF16)
    first_tap = SSD_HIST_ROWS - (SSD_CONV_WIDTH - 1)
    for c0 in range(0, SSD_XBC, PROJ_COL_BLOCK):
        cs = slice(c0, c0 + PROJ_COL_BLOCK)
        xb = jnp.dot(hb, wxbc_ref[:, cs], preferred_element_type=F32)
        for r in range(bb):
            rs = slice(r * tt, (r + 1) * tt)
            xbuf[r, SSD_HIST_ROWS:SSD_HIST_ROWS + tt, cs] = xb[rs]
            acc = jnp.broadcast_to(scb_ref[:, cs], (tt, PROJ_COL_BLOCK))
            for k in range(SSD_CONV_WIDTH):
                acc = acc + scw_ref[k:k + 1, cs] * xbuf[r, first_tap + k:first_tap + k + tt, cs]
            if c0 < D_INNER:
                xs_buf[rs, cs] = _silu(acc)
            else:
                bc_buf[rs, c0 - D_INNER:c0 - D_INNER + PROJ_COL_BLOCK] = _silu(acc)
    for c0 in range(0, D_INNER, PROJ_COL_BLOCK):
        cs = slice(c0, c0 + PROJ_COL_BLOCK)
        zg_buf[:, cs] = _silu(jnp.dot(hb, wz_ref[:, cs], preferred_element_type=F32))
    for c0 in range(0, D_MODEL, PROJ_COL_BLOCK):
        cs = slice(c0, c0 + PROJ_COL_BLOCK)
        gate_buf[:, cs] = _sigmoid(jnp.dot(hb, wg_ref[:, cs], preferred_element_type=F32) + gb_ref[:, cs])

    dt = _softplus(jnp.dot(hb, wdt_ref[...], preferred_element_type=F32) + dtb_ref[...])
    acs = _dot_exact_rhs(tril_ref[...], dt * (-jnp.exp(alog_ref[...])))
    acsx_buf[...] = _expand_heads(acs, e_ref, 3)
    xdt_buf[...] = xs_buf[...] * _expand_heads(dt, e_ref, 2)

    acs_t = jnp.transpose(acs)
    for ci in range(rows // L):
        pieces = []
        for j in range(SSD_HEADS // 2):
            pieces.append(jnp.concatenate([acs_t[2 * j:2 * j + 1, ci * L:(ci + 1) * L],
                                           acs_t[2 * j + 1:2 * j + 2, ci * L:(ci + 1) * L]], axis=-1))
        arow_buf[ci:ci + 1, :] = jnp.concatenate(pieces, axis=-1)

    quad = 4 * SSD_HEAD_DIM
    lane = lax.broadcasted_iota(jnp.int32, (L, quad), 1)
    causal = lax.broadcasted_iota(jnp.int32, (L, quad), 0) >= (lane & (L - 1))
    head_of_lane = lane // SSD_HEAD_DIM

    def chunk_body(ci, carry):
        r0 = pl.multiple_of(ci * L, L)
        cr = pl.ds(r0, L)
        ax = acsx_buf[cr, :]
        arow = arow_buf[pl.ds(ci, 1), :]
        total = acsx_buf[pl.ds(r0 + L - 1, 1), :]
        xdt = xdt_buf[cr, :]
        xw = xdt * jnp.exp(total - ax)
        dfs = jnp.exp(ax)
        cd = jnp.exp(total)
        for g in range(SSD_GROUPS):
            gs0 = g * SSD_HPG * SSD_HEAD_DIM
            gs = slice(gs0, gs0 + SSD_HPG * SSD_HEAD_DIM)
            bg = bc_buf[cr, g * SSD_STATE:(g + 1) * SSD_STATE].astype(BF16)
            cg = bc_buf[cr, SSD_BC + g * SSD_STATE:SSD_BC + (g + 1) * SSD_STATE].astype(BF16)
            cb4 = _dot_nt(cg, jnp.concatenate([bg] * 4, axis=0))
            sg = state[ci // chunks_per_row, :, gs]
            y_off = _dot(cg, sg) * dfs[:, gs]
            for q in range(SSD_HPG // 4):
                qs = slice(gs0 + q * quad, gs0 + (q + 1) * quad)
                gm = jnp.where(causal, jnp.exp(ax[:, qs] - arow[:, qs]), 0.0) * cb4
                xq = xdt[:, qs].astype(BF16)
                blockdiag = jnp.concatenate(
                    [jnp.where(head_of_lane == i, xq, jnp.zeros_like(xq)) for i in range(4)], axis=0)
                y_buf[cr, qs] = (jnp.dot(gm.astype(BF16), blockdiag, preferred_element_type=F32)
                                 + y_off[:, q * quad:(q + 1) * quad])
            state[ci // chunks_per_row, :, gs] = sg * cd[:, gs] + _dot_tn(bg, xw[:, gs])
        return carry

    lax.fori_loop(0, rows // L, chunk_body, 0, unroll=2 if (rows // L) % 2 == 0 else 1)

    y = y_buf[...] + xs_buf[...] * dfull_ref[...]
    y = y * zg_buf[...]
    gw = D_INNER // SSD_GROUPS
    parts = []
    for g in range(SSD_GROUPS):
        yg = y[:, g * gw:(g + 1) * gw]
        parts.append(yg * lax.rsqrt(jnp.mean(yg * yg, axis=-1, keepdims=True) + EPS))
    y = jnp.concatenate(parts, axis=-1) * nw_ref[...]
    out_b = _dot(y, wout_ref[...])
    mb_ref[...] = (gate_buf[...] * out_b).reshape(bb, tt, D_MODEL)

    for r in range(bb):
        xbuf[r, 0:SSD_HIST_ROWS, :] = xbuf[r, tt:tt + SSD_HIST_ROWS, :]

    @pl.when(t == nt - 1)
    def _():
        for r in range(bb):
            newssd_ref[r] = xbuf[r, 0:SSD_HIST_ROWS, :]
            newh_ref[r] = jnp.transpose(state[r]).reshape(SSD_HEADS, SSD_HEAD_DIM, SSD_STATE)


def _head_expand_matrix():
    e = np.zeros((LANES, D_INNER), np.float32)
    for h in range(SSD_HEADS):
        e[h, h * SSD_HEAD_DIM:(h + 1) * SSD_HEAD_DIM] = 1.0
    return jnp.asarray(np.concatenate([e, e, e], axis=0), BF16)


def _chunk_cumsum_matrix(tt):
    r = np.arange(tt)
    same_chunk = (r[:, None] // SCAN_CHUNK) == (r[None, :] // SCAN_CHUNK)
    return jnp.asarray((same_chunk & (r[:, None] >= r[None, :])).astype(np.float32), BF16)


def _ssd_branch(x, st_ssd, st_h, layer, npw, wz, wxbc, wdt, wg, gb, scw, scb, dtb, alog, dfull, nw, wout):
    b, t, _ = x.shape
    tt = _time_tile(t)
    bb = _rows_per_step(b, tt, most=SSD_MAX_ROWS_PER_STEP)
    nt = t // tt
    rows = bb * tt
    assert tt % SCAN_CHUNK == 0 and rows // SCAN_CHUNK <= SUBLANES
    const = lambda i, j: (0, 0)
    wspec = lambda shape: pl.BlockSpec(shape, const, pipeline_mode=pl.Buffered(1))
    sst_pad = jnp.pad(st_ssd, ((0, 0), (SSD_HIST_ROWS - (SSD_CONV_WIDTH - 1), 0), (0, 0)))
    mb, newssd, newh = pl.pallas_call(
        functools.partial(_ssd_branch_kernel, bb=bb, tt=tt, nt=nt),
        out_shape=(jax.ShapeDtypeStruct((b, t, D_MODEL), F32),
                   jax.ShapeDtypeStruct((b, SSD_HIST_ROWS, SSD_XBC), F32),
                   jax.ShapeDtypeStruct((b, SSD_HEADS, SSD_HEAD_DIM, SSD_STATE), F32)),
        grid=(b // bb, nt),
        in_specs=[pl.BlockSpec((bb, tt, D_MODEL), lambda i, j: (i, j, 0)),
                  pl.BlockSpec((bb, SSD_HIST_ROWS, SSD_XBC), lambda i, j: (i, 0, 0)),
                  pl.BlockSpec((1, bb, SSD_HEADS, SSD_HEAD_DIM, SSD_STATE), lambda i, j: (layer, i, 0, 0, 0)),
                  wspec((1, D_MODEL)),
                  wspec((D_MODEL, D_INNER)),
                  wspec((D_MODEL, SSD_XBC)),
                  wspec((D_MODEL, LANES)),
                  wspec((D_MODEL, D_MODEL)),
                  wspec((1, D_MODEL)),
                  wspec((SSD_CONV_WIDTH, SSD_XBC)),
                  wspec((1, SSD_XBC)),
                  wspec((1, LANES)),
                  wspec((1, LANES)),
                  wspec((1, D_INNER)),
                  wspec((1, D_INNER)),
                  wspec((D_INNER, D_MODEL)),
                  wspec((3 * LANES, D_INNER)),
                  wspec((rows, rows))],
        out_specs=(pl.BlockSpec((bb, tt, D_MODEL), lambda i, j: (i, j, 0)),
                   pl.BlockSpec((bb, SSD_HIST_ROWS, SSD_XBC), lambda i, j: (i, 0, 0)),
                   pl.BlockSpec((bb, SSD_HEADS, SSD_HEAD_DIM, SSD_STATE), lambda i, j: (i, 0, 0, 0))),
        scratch_shapes=[pltpu.VMEM((bb, tt + SSD_HIST_ROWS, SSD_XBC), F32),
                        pltpu.VMEM((bb, SSD_STATE, D_INNER), F32),
                        pltpu.VMEM((rows, D_INNER), F32),
                        pltpu.VMEM((rows, 2 * SSD_BC), F32),
                        pltpu.VMEM((rows, D_INNER), F32),
                        pltpu.VMEM((rows, D_INNER), F32),
                        pltpu.VMEM((SUBLANES, D_INNER), F32),
                        pltpu.VMEM((rows, D_INNER), F32),
                        pltpu.VMEM((rows, D_INNER), F32),
                        pltpu.VMEM((rows, D_MODEL), F32)],
        compiler_params=pltpu.CompilerParams(dimension_semantics=("arbitrary", "arbitrary"),
                                             vmem_limit_bytes=VMEM_LIMIT_BYTES),
        name="ssd_branch",
    )(x, sst_pad, st_h, npw, wz, wxbc, wdt, wg, gb, scw, scb, dtb, alog, dfull, nw, wout, _head_expand_matrix(),
      _chunk_cumsum_matrix(rows))
    return mb, newssd[:, SSD_HIST_ROWS - (SSD_CONV_WIDTH - 1):, :], newh


def _attn_merge_kernel(x_ref, ma_ref, mb_ref, k_ref, v_ref, npw_ref, wq_ref, wxg_ref, wg_ref, gb_ref,
                       xow_ref, wo_ref, npost_ref, y_ref, *, bb, tt):
    rows = bb * tt
    x = x_ref[...].reshape(rows, D_MODEL)
    hb = _rms(x, npw_ref[...]).astype(BF16)
    q = jnp.dot(hb, wq_ref[...], preferred_element_type=F32)
    per_row = []
    for r in range(bb):
        kb = k_ref[0, r].astype(BF16)
        vb = v_ref[0, r].astype(BF16)
        heads = []
        for hd in range(XA_HEADS):
            sl = slice(hd * XA_HEAD_DIM, (hd + 1) * XA_HEAD_DIM)
            s = _dot_nt(q[r * tt:(r + 1) * tt, sl], kb[:, sl]) * (XA_HEAD_DIM ** -0.5)
            e = jnp.exp(s - jnp.max(s, axis=-1, keepdims=True))
            p = e / jnp.sum(e, axis=-1, keepdims=True)
            heads.append(_dot(p, vb[:, sl]))
        per_row.append(jnp.concatenate(heads, axis=-1))
    o = jnp.concatenate(per_row, axis=0)
    o = o * _silu(jnp.dot(hb, wxg_ref[...], preferred_element_type=F32))
    out_c = _dot(o, xow_ref[...])
    gate = _sigmoid(jnp.dot(hb, wg_ref[...], preferred_element_type=F32) + gb_ref[...])
    m = ma_ref[...].reshape(rows, D_MODEL) + mb_ref[...].reshape(rows, D_MODEL) + gate * out_c
    y_ref[...] = (x + _rms(_dot(m, wo_ref[...]), npost_ref[...])).reshape(bb, tt, D_MODEL)


def _attn_merge(x, ma, mb, mem_k, mem_v, layer, npw, wq, wxg, wg, gb, xow, wo, npost):
    b, t, _ = x.shape
    tt = _time_tile(t)
    bb = _rows_per_step(b, tt)
    nt = t // tt
    const = lambda i, j: (0, 0)
    wspec = lambda shape: pl.BlockSpec(shape, const, pipeline_mode=pl.Buffered(1))
    tile = pl.BlockSpec((bb, tt, D_MODEL), lambda i, j: (i, j, 0))
    mem = pl.BlockSpec((1, bb, N_MEM, XA_DIM), lambda i, j: (layer, i, 0, 0))
    return pl.pallas_call(
        functools.partial(_attn_merge_kernel, bb=bb, tt=tt),
        out_shape=jax.ShapeDtypeStruct((b, t, D_MODEL), F32),
        grid=(b // bb, nt),
        in_specs=[tile, tile, tile, mem, mem,
                  wspec((1, D_MODEL)),
                  wspec((D_MODEL, XA_DIM)),
                  wspec((D_MODEL, XA_DIM)),
                  wspec((D_MODEL, D_MODEL)),
                  wspec((1, D_MODEL)),
                  wspec((XA_DIM, D_MODEL)),
                  wspec((D_MODEL, D_MODEL)),
                  wspec((1, D_MODEL))],
        out_specs=tile,
        compiler_params=pltpu.CompilerParams(dimension_semantics=("arbitrary", "arbitrary"),
                                             vmem_limit_bytes=VMEM_LIMIT_BYTES),
        name="attn_merge",
    )(x, ma, mb, mem_k, mem_v, npw, wq, wxg, wg, gb, xow, wo, npost)


def _layer_weights(l, norm_pre_w, w_in, gate_b, conv_dw_w, conv_dw_b, conv_ln_w, conv_ln_b, conv_out_w,
                   ssd_conv_w, ssd_conv_b, ssd_dt_bias, ssd_a_log, ssd_d, ssd_norm_w, ssd_out_w,
                   xa_out_w, w_out, norm_post_w):
    offs = np.cumsum((0,) + IN_SIZES)
    seg = lambda i: w_in[l][:, offs[i]:offs[i + 1]].astype(BF16)
    row = lambda v: v.reshape(1, -1)
    pad_lanes = lambda v: jnp.pad(v, ((0, 0), (0, LANES - v.shape[-1])))
    gates_w = seg(8)
    gb = gate_b[l]
    return dict(
        npw=row(norm_pre_w[l]),
        w3=w_in[l][:, offs[0]:offs[3]].astype(BF16),
        wz=seg(3), wxbc=seg(4), wdt=pad_lanes(seg(5)), wq=seg(6), wxg=seg(7),
        wg=[gates_w[:, i * D_MODEL:(i + 1) * D_MODEL] for i in range(N_BRANCH)],
        gb=[row(gb[i * D_MODEL:(i + 1) * D_MODEL]) for i in range(N_BRANCH)],
        cw=conv_dw_w[l], cb=row(conv_dw_b[l]), lnw=row(conv_ln_w[l]), lnb=row(conv_ln_b[l]),
        cow=conv_out_w[l].astype(BF16),
        scw=ssd_conv_w[l], scb=row(ssd_conv_b[l]),
        dtb=pad_lanes(row(ssd_dt_bias[l])), alog=pad_lanes(row(ssd_a_log[l])),
        dfull=row(jnp.repeat(ssd_d[l], SSD_HEAD_DIM)), nw=row(ssd_norm_w[l]),
        wout=ssd_out_w[l].astype(BF16),
        xow=xa_out_w[l].astype(BF16), wo=w_out[l].astype(BF16), npost=row(norm_post_w[l]),
    )


def _layer(x, mem_k, mem_v, mem_layer, st_a, st_ssd, st_h, state_layer, w):
    ma, new_a = _conv_branch(x, st_a, w["npw"], w["w3"], w["wg"][0], w["gb"][0], w["cw"], w["cb"],
                             w["lnw"], w["lnb"], w["cow"])
    mb, new_ssd, new_h = _ssd_branch(x, st_ssd, st_h, state_layer, w["npw"], w["wz"], w["wxbc"], w["wdt"], w["wg"][1],
                                     w["gb"][1], w["scw"], w["scb"], w["dtb"], w["alog"], w["dfull"],
                                     w["nw"], w["wout"])
    y = _attn_merge(x, ma, mb, mem_k, mem_v, mem_layer, w["npw"], w["wq"], w["wxg"], w["wg"][2], w["gb"][2],
                    w["xow"], w["wo"], w["npost"])
    return y, new_a, new_ssd, new_h


def kernel(x_prompt, x_sample, mem_prompt, cache_mem_k, cache_mem_v, state_conv_a, state_conv_ssd, state_ssm,
           norm_pre_w, w_in, gate_b, conv_dw_w, conv_dw_b, conv_ln_w, conv_ln_b, conv_out_w,
           ssd_conv_w, ssd_conv_b, ssd_dt_bias, ssd_a_log, ssd_d, ssd_norm_w, ssd_out_w,
           mem_norm_w, xa_kv_w, xa_out_w, w_out, norm_post_w):
    bp = x_prompt.shape[0]
    bs = x_sample.shape[0]
    weights = [_layer_weights(l, norm_pre_w, w_in, gate_b, conv_dw_w, conv_dw_b, conv_ln_w, conv_ln_b,
                              conv_out_w, ssd_conv_w, ssd_conv_b, ssd_dt_bias, ssd_a_log, ssd_d, ssd_norm_w,
                              ssd_out_w, xa_out_w, w_out, norm_post_w) for l in range(DEPTH)]

    y_p = x_prompt
    mk_p, mv_p = _mem_kv(mem_prompt, mem_norm_w, xa_kv_w)
    zero_h = jnp.zeros((1, bp, SSD_HEADS, SSD_HEAD_DIM, SSD_STATE), F32)
    ca_p, cs_p, hs_p = [], [], []
    for l in range(DEPTH):
        y_p, na, ns, nh = _layer(y_p, mk_p, mv_p, l,
                                 jnp.zeros((bp, CONV_WIDTH - 1, CONV_DIM), F32),
                                 jnp.zeros((bp, SSD_CONV_WIDTH - 1, SSD_XBC), F32),
                                 zero_h, 0, weights[l])
        ca_p.append(na); cs_p.append(ns); hs_p.append(nh)

    y_s = x_sample
    mk_s = cache_mem_k.reshape(DEPTH, bs, N_MEM, XA_DIM)
    mv_s = cache_mem_v.reshape(DEPTH, bs, N_MEM, XA_DIM)
    ca_s, cs_s, hs_s = [], [], []
    for l in range(DEPTH):
        y_s, na, ns, nh = _layer(y_s, mk_s, mv_s, l, state_conv_a[l], state_conv_ssd[l], state_ssm, l, weights[l])
        ca_s.append(na); cs_s.append(ns); hs_s.append(nh)

    return (y_p, y_s,
            mk_p.reshape(DEPTH, bp, N_MEM, XA_HEADS, XA_HEAD_DIM),
            mv_p.reshape(DEPTH, bp, N_MEM, XA_HEADS, XA_HEAD_DIM),
            jnp.stack(ca_p), jnp.stack(cs_p), jnp.stack(hs_p),
            jnp.stack(ca_s), jnp.stack(cs_s), jnp.stack(hs_s))
```

```python
import functools

import jax
import jax.numpy as jnp
import numpy as np
from jax import lax
from jax.experimental import pallas as pl
from jax.experimental.pallas import tpu as pltpu

D_MODEL = 1024
DEPTH = 2
N_MEM = 256
CONV_DIM = D_MODEL
CONV_WIDTH = 31
D_INNER = 2 * D_MODEL
SSD_HEAD_DIM = 64
SSD_HEADS = D_INNER // SSD_HEAD_DIM
SSD_GROUPS = 4
SSD_HPG = SSD_HEADS // SSD_GROUPS
SSD_STATE = 128
SSD_CONV_WIDTH = 4
SSD_BC = SSD_GROUPS * SSD_STATE
SSD_XBC = D_INNER + 2 * SSD_BC
XA_HEADS = 4
XA_HEAD_DIM = D_MODEL // XA_HEADS
XA_DIM = XA_HEADS * XA_HEAD_DIM
N_BRANCH = 3
IN_SIZES = (CONV_DIM, CONV_DIM, CONV_DIM, D_INNER, SSD_XBC, SSD_HEADS, XA_DIM, XA_DIM, N_BRANCH * D_MODEL)
EPS = 1e-6

LANES = 128
SUBLANES = 8
SCAN_CHUNK = 64
CONV_HIST_ROWS = 32
SSD_HIST_ROWS = 8
CONV_ROW_BLOCK = 64
MXU_ROWS_TARGET = 256
SSD_MAX_ROWS_PER_STEP = 2
PROJ_COL_BLOCK = 512
CONV_TAP_GROUP = 8
VMEM_LIMIT_BYTES = 56 * 1024 * 1024

F32 = jnp.float32
BF16 = jnp.bfloat16


def _time_tile(t):
    return 256 if t % 256 == 0 else t


def _rows_per_step(b, tt, most=None):
    bb = max(1, MXU_ROWS_TARGET // tt)
    if most is not None:
        bb = min(bb, most)
    while b % bb:
        bb -= 1
    return bb


def _rms(x, w):
    return x * lax.rsqrt(jnp.mean(x * x, axis=-1, keepdims=True) + EPS) * w


def _sigmoid(x):
    return 0.5 + 0.5 * jnp.tanh(0.5 * x)


def _silu(x):
    hx = 0.5 * x
    return hx + hx * jnp.tanh(hx)


def _softplus(x):
    return jnp.maximum(x, 0.0) + jnp.log1p(jnp.exp(-jnp.abs(x)))


def _dot(a, b):
    return jnp.dot(a.astype(BF16), b.astype(BF16), preferred_element_type=F32)


def _dot_nt(a, b):
    return lax.dot_general(a.astype(BF16), b.astype(BF16), (((1,), (1,)), ((), ())),
                           preferred_element_type=F32)


def _dot_tn(a, b):
    return lax.dot_general(a.astype(BF16), b.astype(BF16), (((0,), (0,)), ((), ())),
                           preferred_element_type=F32)


def _split3(v):
    hi = v.astype(BF16)
    r1 = v - hi.astype(F32)
    mid = r1.astype(BF16)
    lo = (r1 - mid.astype(F32)).astype(BF16)
    return hi, mid, lo


def _dot_exact_rhs(a_bf16, v):
    hi, mid, lo = _split3(v)
    return (jnp.dot(a_bf16, hi, preferred_element_type=F32)
            + jnp.dot(a_bf16, mid, preferred_element_type=F32)
            + jnp.dot(a_bf16, lo, preferred_element_type=F32))


def _expand_heads(v, e_ref, parts):
    split = _split3(v)[:parts]
    return jnp.dot(jnp.concatenate(split, axis=-1), e_ref[0:parts * LANES, :], preferred_element_type=F32)


def _memkv_kernel(mem_ref, nw_ref, w_ref, k_ref, v_ref):
    h = _rms(mem_ref[0], nw_ref[0])
    kv = _dot(h, w_ref[0])
    k_ref[0, 0] = kv[:, :XA_DIM]
    v_ref[0, 0] = kv[:, XA_DIM:]


def _mem_kv(mem, norm_w, kv_w):
    b = mem.shape[0]
    layers = kv_w.shape[0]
    out = pl.BlockSpec((1, 1, N_MEM, XA_DIM), lambda l, i: (l, i, 0, 0))
    return pl.pallas_call(
        _memkv_kernel,
        out_shape=(jax.ShapeDtypeStruct((layers, b, N_MEM, XA_DIM), F32),
                   jax.ShapeDtypeStruct((layers, b, N_MEM, XA_DIM), F32)),
        grid=(layers, b),
        in_specs=[pl.BlockSpec((1, N_MEM, D_MODEL), lambda l, i: (i, 0, 0)),
                  pl.BlockSpec((1, 1, D_MODEL), lambda l, i: (l, 0, 0)),
                  pl.BlockSpec((1, D_MODEL, 2 * XA_DIM), lambda l, i: (l, 0, 0))],
        out_specs=(out, out),
        compiler_params=pltpu.CompilerParams(dimension_semantics=("arbitrary", "arbitrary"),
                                             vmem_limit_bytes=VMEM_LIMIT_BYTES),
        name="mem_kv",
    )(mem, norm_w.reshape(layers, 1, D_MODEL), kv_w.astype(BF16))


def _time_permutation(tt):
    n = tt // SUBLANES
    perm = np.zeros((tt, tt), np.float32)
    hist = np.zeros((tt, CONV_HIST_ROWS), np.float32)
    for rho in range(tt):
        i, s = divmod(rho, SUBLANES)
        perm[rho, s * n + i] = 1.0
    for m in range(CONV_HIST_ROWS):
        s = SUBLANES - CONV_HIST_ROWS // n + m // n
        hist[(m % n) * SUBLANES + s, m] = 1.0
    return perm, hist


def _conv_branch_kernel(x_ref, st_ref, perm_ref, permt_ref, hist_ref, histt_ref, npw_ref, w3_ref, wg_ref, gb_ref,
                        cw_ref, cb_ref, lnw_ref, lnb_ref, cow_ref, ma_ref, newa_ref, ext, cbuf,
                        *, bb, tt, nt, wraps):
    t = pl.program_id(1)
    n = tt // SUBLANES
    rows = bb * tt
    span = (wraps + 1) * tt
    nslab = CONV_DIM // LANES

    def put(ref, r0, val):
        for c in range(nslab):
            ref[c, r0:r0 + val.shape[0], :] = val[:, c * LANES:(c + 1) * LANES]

    def get(ref, r0, nrows):
        return jnp.concatenate([ref[c, r0:r0 + nrows, :] for c in range(nslab)], axis=-1)

    @pl.when(t == 0)
    def _():
        for r in range(bb):
            put(ext, r * span + wraps * tt, _dot_exact_rhs(hist_ref[...], st_ref[r]))

    hb = _rms(x_ref[...].reshape(rows, D_MODEL), npw_ref[...]).astype(BF16)
    hb_p = jnp.dot(perm_ref[...], hb, preferred_element_type=F32).astype(BF16)
    glu_v = jnp.dot(hb_p, w3_ref[:, 0:CONV_DIM], preferred_element_type=F32)
    glu_g = jnp.dot(hb_p, w3_ref[:, CONV_DIM:2 * CONV_DIM], preferred_element_type=F32)
    u = glu_v * _sigmoid(glu_g)

    sub = lax.broadcasted_iota(jnp.int32, (n, SUBLANES, CONV_DIM), 1)
    for r in range(bb):
        cur = r * span + wraps * tt
        prev3 = get(ext, cur, tt).reshape(n, SUBLANES, CONV_DIM)
        u3 = u[r * tt:(r + 1) * tt].reshape(n, SUBLANES, CONV_DIM)
        for q in range(1, wraps + 1):
            mixed = jnp.where(sub >= SUBLANES - q, prev3, u3)
            put(ext, r * span + (wraps - q) * tt, pltpu.roll(mixed, q, 1).reshape(tt, CONV_DIM))
        put(ext, cur, u[r * tt:(r + 1) * tt])

    vregs_per_block = CONV_ROW_BLOCK // SUBLANES
    blocks_per_row = tt // CONV_ROW_BLOCK
    blocks_per_slab = bb * blocks_per_row

    def conv_block(idx, carry):
        c = idx // blocks_per_slab
        blk = idx % blocks_per_slab
        r = blk // blocks_per_row
        r0 = (blk % blocks_per_row) * CONV_ROW_BLOCK
        first = r * span + wraps * tt + r0 - SUBLANES * (CONV_WIDTH - 1)
        acc = [jnp.broadcast_to(cb_ref[c], (SUBLANES, LANES))] * vregs_per_block
        for k0 in range(0, CONV_WIDTH, CONV_TAP_GROUP):
            taps = range(k0, min(k0 + CONV_TAP_GROUP, CONV_WIDTH))
            wk = [jnp.broadcast_to(cw_ref[c, k:k + 1, :], (SUBLANES, LANES)) for k in taps]
            win = [ext[c, pl.ds(pl.multiple_of(first + SUBLANES * (k0 + j), SUBLANES), SUBLANES), :]
                   for j in range(vregs_per_block + len(taps) - 1)]
            for m in range(vregs_per_block):
                for j in range(len(taps)):
                    acc[m] = acc[m] + wk[j] * win[m + j]
        for m in range(vregs_per_block):
            cbuf[c, pl.ds(pl.multiple_of(r * tt + r0 + SUBLANES * m, SUBLANES), SUBLANES), :] = acc[m]
        return carry

    lax.fori_loop(0, nslab * blocks_per_slab, conv_block, 0, unroll=2)

    c = get(cbuf, 0, rows)
    mu = jnp.mean(c, axis=-1, keepdims=True)
    xc = c - mu
    y = xc * lax.rsqrt(jnp.mean(xc * xc, axis=-1, keepdims=True) + EPS) * lnw_ref[...] + lnb_ref[...]
    conv_gate = jnp.dot(hb_p, w3_ref[:, 2 * CONV_DIM:3 * CONV_DIM], preferred_element_type=F32)
    y = (_silu(y) * _silu(conv_gate)).astype(BF16)
    y = jnp.dot(permt_ref[...], y, preferred_element_type=F32).astype(BF16)
    out_a = jnp.dot(y, cow_ref[...], preferred_element_type=F32)
    gate = _sigmoid(jnp.dot(hb, wg_ref[...], preferred_element_type=F32) + gb_ref[...])
    ma_ref[...] = (gate * out_a).reshape(bb, tt, D_MODEL)

    @pl.when(t == nt - 1)
    def _():
        for r in range(bb):
            newa_ref[r] = _dot_exact_rhs(histt_ref[...], get(ext, r * span + wraps * tt, tt))


def _conv_branch(x, st_a, npw, w3, wg, gb, cw, cb, lnw, lnb, cow):
    b, t, _ = x.shape
    tt = _time_tile(t)
    bb = _rows_per_step(b, tt)
    nt = t // tt
    n = tt // SUBLANES
    rows = bb * tt
    assert tt % SUBLANES == 0 and CONV_HIST_ROWS % n == 0 and CONV_HIST_ROWS // n <= SUBLANES
    wraps = -(-(CONV_WIDTH - 1) // n)
    assert wraps < SUBLANES and wraps * tt >= SUBLANES * (CONV_WIDTH - 1)
    perm, hist = _time_permutation(tt)
    perm = np.kron(np.eye(bb, dtype=np.float32), perm)
    nslab = CONV_DIM // LANES
    assert tt % CONV_ROW_BLOCK == 0
    const = lambda i, j: (0, 0)
    wspec = lambda shape: pl.BlockSpec(shape, const, pipeline_mode=pl.Buffered(1))
    st_pad = jnp.pad(st_a, ((0, 0), (CONV_HIST_ROWS - (CONV_WIDTH - 1), 0), (0, 0)))
    ma, newa = pl.pallas_call(
        functools.partial(_conv_branch_kernel, bb=bb, tt=tt, nt=nt, wraps=wraps),
        out_shape=(jax.ShapeDtypeStruct((b, t, D_MODEL), F32),
                   jax.ShapeDtypeStruct((b, CONV_HIST_ROWS, CONV_DIM), F32)),
        grid=(b // bb, nt),
        in_specs=[pl.BlockSpec((bb, tt, D_MODEL), lambda i, j: (i, j, 0)),
                  pl.BlockSpec((bb, CONV_HIST_ROWS, CONV_DIM), lambda i, j: (i, 0, 0)),
                  wspec((rows, rows)),
                  wspec((rows, rows)),
                  wspec((tt, CONV_HIST_ROWS)),
                  wspec((CONV_HIST_ROWS, tt)),
                  wspec((1, D_MODEL)),
                  wspec((D_MODEL, 3 * CONV_DIM)),
                  wspec((D_MODEL, D_MODEL)),
                  wspec((1, D_MODEL)),
                  pl.BlockSpec((nslab, CONV_WIDTH, LANES), lambda i, j: (0, 0, 0), pipeline_mode=pl.Buffered(1)),
                  pl.BlockSpec((nslab, 1, LANES), lambda i, j: (0, 0, 0), pipeline_mode=pl.Buffered(1)),
                  wspec((1, CONV_DIM)),
                  wspec((1, CONV_DIM)),
                  wspec((CONV_DIM, D_MODEL))],
        out_specs=(pl.BlockSpec((bb, tt, D_MODEL), lambda i, j: (i, j, 0)),
                   pl.BlockSpec((bb, CONV_HIST_ROWS, CONV_DIM), lambda i, j: (i, 0, 0))),
        scratch_shapes=[pltpu.VMEM((nslab, bb * (wraps + 1) * tt, LANES), F32),
                        pltpu.VMEM((nslab, rows, LANES), F32)],
        compiler_params=pltpu.CompilerParams(dimension_semantics=("arbitrary", "arbitrary"),
                                             vmem_limit_bytes=VMEM_LIMIT_BYTES),
        name="conv_branch",
    )(x, st_pad, jnp.asarray(perm, BF16), jnp.asarray(perm.T, BF16), jnp.asarray(hist, BF16),
      jnp.asarray(hist.T, BF16), npw, w3, wg, gb,
      jnp.transpose(cw.reshape(CONV_WIDTH, nslab, LANES), (1, 0, 2)), cb.reshape(nslab, 1, LANES),
      lnw, lnb, cow)
    return ma, newa[:, CONV_HIST_ROWS - (CONV_WIDTH - 1):, :]


def _ssd_branch_kernel(x_ref, sst_ref, sh_ref, npw_ref, wz_ref, wxbc_ref, wdt_ref, wg_ref, gb_ref,
                       scw_ref, scb_ref, dtb_ref, alog_ref, dfull_ref, nw_ref, wout_ref, e_ref, tril_ref,
                       mb_ref, newssd_ref, newh_ref,
                       xbuf, state, xs_buf, bc_buf, acsx_buf, xdt_buf, arow_buf, y_buf, zg_buf, gate_buf,
                       *, bb, tt, nt):
    t = pl.program_id(1)
    L = SCAN_CHUNK
    rows = bb * tt
    chunks_per_row = tt // L

    @pl.when(t == 0)
    def _():
        for r in range(bb):
            xbuf[r, 0:SSD_HIST_ROWS, :] = sst_ref[r]
            state[r] = jnp.transpose(sh_ref[0, r].reshape(D_INNER, SSD_STATE))

    hb = _rms(x_ref[...].reshape(rows, D_MODEL), npw_ref[...]).astype(BF16)
    first_tap = SSD_HIST_ROWS - (SSD_CONV_WIDTH - 1)
    for c0 in range(0, SSD_XBC, PROJ_COL_BLOCK):
        cs = slice(c0, c0 + PROJ_COL_BLOCK)
        xb = jnp.dot(hb, wxbc_ref[:, cs], preferred_element_type=F32)
        for r in range(bb):
            rs = slice(r * tt, (r + 1) * tt)
            xbuf[r, SSD_HIST_ROWS:SSD_HIST_ROWS + tt, cs] = xb[rs]
            acc = jnp.broadcast_to(scb_ref[:, cs], (tt, PROJ_COL_BLOCK))
            for k in range(SSD_CONV_WIDTH):
                acc = acc + scw_ref[k:k + 1, cs] * xbuf[r, first_tap + k:first_tap + k + tt, cs]
            if c0 < D_INNER:
                xs_buf[rs, cs] = _silu(acc)
            else:
                bc_buf[rs, c0 - D_INNER:c0 - D_INNER + PROJ_COL_BLOCK] = _silu(acc)
    for c0 in range(0, D_INNER, PROJ_COL_BLOCK):
        cs = slice(c0, c0 + PROJ_COL_BLOCK)
        zg_buf[:, cs] = _silu(jnp.dot(hb, wz_ref[:, cs], preferred_element_type=F32))
    for c0 in range(0, D_MODEL, PROJ_COL_BLOCK):
        cs = slice(c0, c0 + PROJ_COL_BLOCK)
        gate_buf[:, cs] = _sigmoid(jnp.dot(hb, wg_ref[:, cs], preferred_element_type=F32) + gb_ref[:, cs])

    dt = _softplus(jnp.dot(hb, wdt_ref[...], preferred_element_type=F32) + dtb_ref[...])
    acs = _dot_exact_rhs(tril_ref[...], dt * (-jnp.exp(alog_ref[...])))
    acsx_buf[...] = _expand_heads(acs, e_ref, 3)
    xdt_buf[...] = xs_buf[...] * _expand_heads(dt, e_ref, 2)

    acs_t = jnp.transpose(acs)
    for ci in range(rows // L):
        pieces = []
        for j in range(SSD_HEADS // 2):
            pieces.append(jnp.concatenate([acs_t[2 * j:2 * j + 1, ci * L:(ci + 1) * L],
                                           acs_t[2 * j + 1:2 * j + 2, ci * L:(ci + 1) * L]], axis=-1))
        arow_buf[ci:ci + 1, :] = jnp.concatenate(pieces, axis=-1)

    quad = 4 * SSD_HEAD_DIM
    lane = lax.broadcasted_iota(jnp.int32, (L, quad), 1)
    causal = lax.broadcasted_iota(jnp.int32, (L, quad), 0) >= (lane & (L - 1))
    head_of_lane = lane // SSD_HEAD_DIM

    def chunk_body(ci, carry):
        r0 = pl.multiple_of(ci * L, L)
        cr = pl.ds(r0, L)
        ax = acsx_buf[cr, :]
        arow = arow_buf[pl.ds(ci, 1), :]
        total = acsx_buf[pl.ds(r0 + L - 1, 1), :]
        xdt = xdt_buf[cr, :]
        xw = xdt * jnp.exp(total - ax)
        dfs = jnp.exp(ax)
        cd = jnp.exp(total)
        for g in range(SSD_GROUPS):
            gs0 = g * SSD_HPG * SSD_HEAD_DIM
            gs = slice(gs0, gs0 + SSD_HPG * SSD_HEAD_DIM)
            bg = bc_buf[cr, g * SSD_STATE:(g + 1) * SSD_STATE].astype(BF16)
            cg = bc_buf[cr, SSD_BC + g * SSD_STATE:SSD_BC + (g + 1) * SSD_STATE].astype(BF16)
            cb4 = _dot_nt(cg, jnp.concatenate([bg] * 4, axis=0))
            sg = state[ci // chunks_per_row, :, gs]
            y_off = _dot(cg, sg) * dfs[:, gs]
            for q in range(SSD_HPG // 4):
                qs = slice(gs0 + q * quad, gs0 + (q + 1) * quad)
                gm = jnp.where(causal, jnp.exp(ax[:, qs] - arow[:, qs]), 0.0) * cb4
                xq = xdt[:, qs].astype(BF16)
                blockdiag = jnp.concatenate(
                    [jnp.where(head_of_lane == i, xq, jnp.zeros_like(xq)) for i in range(4)], axis=0)
                y_buf[cr, qs] = (jnp.dot(gm.astype(BF16), blockdiag, preferred_element_type=F32)
                                 + y_off[:, q * quad:(q + 1) * quad])
            state[ci // chunks_per_row, :, gs] = sg * cd[:, gs] + _dot_tn(bg, xw[:, gs])
        return carry

    lax.fori_loop(0, rows // L, chunk_body, 0, unroll=4 if (rows // L) % 4 == 0 else 2)

    y = y_buf[...] + xs_buf[...] * dfull_ref[...]
    y = y * zg_buf[...]
    gw = D_INNER // SSD_GROUPS
    parts = []
    for g in range(SSD_GROUPS):
        yg = y[:, g * gw:(g + 1) * gw]
        parts.append(yg * lax.rsqrt(jnp.mean(yg * yg, axis=-1, keepdims=True) + EPS))
    y = jnp.concatenate(parts, axis=-1) * nw_ref[...]
    out_b = _dot(y, wout_ref[...])
    mb_ref[...] = (gate_buf[...] * out_b).reshape(bb, tt, D_MODEL)

    for r in range(bb):
        xbuf[r, 0:SSD_HIST_ROWS, :] = xbuf[r, tt:tt + SSD_HIST_ROWS, :]

    @pl.when(t == nt - 1)
    def _():
        for r in range(bb):
            newssd_ref[r] = xbuf[r, 0:SSD_HIST_ROWS, :]
            newh_ref[r] = jnp.transpose(state[r]).reshape(SSD_HEADS, SSD_HEAD_DIM, SSD_STATE)


def _head_expand_matrix():
    e = np.zeros((LANES, D_INNER), np.float32)
    for h in range(SSD_HEADS):
        e[h, h * SSD_HEAD_DIM:(h + 1) * SSD_HEAD_DIM] = 1.0
    return jnp.asarray(np.concatenate([e, e, e], axis=0), BF16)


def _chunk_cumsum_matrix(tt):
    r = np.arange(tt)
    same_chunk = (r[:, None] // SCAN_CHUNK) == (r[None, :] // SCAN_CHUNK)
    return jnp.asarray((same_chunk & (r[:, None] >= r[None, :])).astype(np.float32), BF16)


def _ssd_branch(x, st_ssd, st_h, layer, npw, wz, wxbc, wdt, wg, gb, scw, scb, dtb, alog, dfull, nw, wout):
    b, t, _ = x.shape
    tt = _time_tile(t)
    bb = _rows_per_step(b, tt, most=SSD_MAX_ROWS_PER_STEP)
    nt = t // tt
    rows = bb * tt
    assert tt % SCAN_CHUNK == 0 and rows // SCAN_CHUNK <= SUBLANES
    const = lambda i, j: (0, 0)
    wspec = lambda shape: pl.BlockSpec(shape, const, pipeline_mode=pl.Buffered(1))
    sst_pad = jnp.pad(st_ssd, ((0, 0), (SSD_HIST_ROWS - (SSD_CONV_WIDTH - 1), 0), (0, 0)))
    mb, newssd, newh = pl.pallas_call(
        functools.partial(_ssd_branch_kernel, bb=bb, tt=tt, nt=nt),
        out_shape=(jax.ShapeDtypeStruct((b, t, D_MODEL), F32),
                   jax.ShapeDtypeStruct((b, SSD_HIST_ROWS, SSD_XBC), F32),
                   jax.ShapeDtypeStruct((b, SSD_HEADS, SSD_HEAD_DIM, SSD_STATE), F32)),
        grid=(b // bb, nt),
        in_specs=[pl.BlockSpec((bb, tt, D_MODEL), lambda i, j: (i, j, 0)),
                  pl.BlockSpec((bb, SSD_HIST_ROWS, SSD_XBC), lambda i, j: (i, 0, 0)),
                  pl.BlockSpec((1, bb, SSD_HEADS, SSD_HEAD_DIM, SSD_STATE), lambda i, j: (layer, i, 0, 0, 0)),
                  wspec((1, D_MODEL)),
                  wspec((D_MODEL, D_INNER)),
                  wspec((D_MODEL, SSD_XBC)),
                  wspec((D_MODEL, LANES)),
                  wspec((D_MODEL, D_MODEL)),
                  wspec((1, D_MODEL)),
                  wspec((SSD_CONV_WIDTH, SSD_XBC)),
                  wspec((1, SSD_XBC)),
                  wspec((1, LANES)),
                  wspec((1, LANES)),
                  wspec((1, D_INNER)),
                  wspec((1, D_INNER)),
                  wspec((D_INNER, D_MODEL)),
                  wspec((3 * LANES, D_INNER)),
                  wspec((rows, rows))],
        out_specs=(pl.BlockSpec((bb, tt, D_MODEL), lambda i, j: (i, j, 0)),
                   pl.BlockSpec((bb, SSD_HIST_ROWS, SSD_XBC), lambda i, j: (i, 0, 0)),
                   pl.BlockSpec((bb, SSD_HEADS, SSD_HEAD_DIM, SSD_STATE), lambda i, j: (i, 0, 0, 0))),
        scratch_shapes=[pltpu.VMEM((bb, tt + SSD_HIST_ROWS, SSD_XBC), F32),
                        pltpu.VMEM((bb, SSD_STATE, D_INNER), F32),
                        pltpu.VMEM((rows, D_INNER), F32),
                        pltpu.VMEM((rows, 2 * SSD_BC), F32),
                        pltpu.VMEM((rows, D_INNER), F32),
                        pltpu.VMEM((rows, D_INNER), F32),
                        pltpu.VMEM((SUBLANES, D_INNER), F32),
                        pltpu.VMEM((rows, D_INNER), F32),
                        pltpu.VMEM((rows, D_INNER), F32),
                        pltpu.VMEM((rows, D_MODEL), F32)],
        compiler_params=pltpu.CompilerParams(dimension_semantics=("arbitrary", "arbitrary"),
                                             vmem_limit_bytes=VMEM_LIMIT_BYTES),
        name="ssd_branch",
    )(x, sst_pad, st_h, npw, wz, wxbc, wdt, wg, gb, scw, scb, dtb, alog, dfull, nw, wout, _head_expand_matrix(),
      _chunk_cumsum_matrix(rows))
    return mb, newssd[:, SSD_HIST_ROWS - (SSD_CONV_WIDTH - 1):, :], newh


def _attn_merge_kernel(x_ref, ma_ref, mb_ref, k_ref, v_ref, npw_ref, wq_ref, wxg_ref, wg_ref, gb_ref,
                       xow_ref, wo_ref, npost_ref, y_ref, *, bb, tt):
    rows = bb * tt
    x = x_ref[...].reshape(rows, D_MODEL)
    hb = _rms(x, npw_ref[...]).astype(BF16)
    q = jnp.dot(hb, wq_ref[...], preferred_element_type=F32)
    per_row = []
    for r in range(bb):
        kb = k_ref[0, r].astype(BF16)
        vb = v_ref[0, r].astype(BF16)
        heads = []
        for hd in range(XA_HEADS):
            sl = slice(hd * XA_HEAD_DIM, (hd + 1) * XA_HEAD_DIM)
            s = _dot_nt(q[r * tt:(r + 1) * tt, sl], kb[:, sl]) * (XA_HEAD_DIM ** -0.5)
            e = jnp.exp(s - jnp.max(s, axis=-1, keepdims=True))
            p = e / jnp.sum(e, axis=-1, keepdims=True)
            heads.append(_dot(p, vb[:, sl]))
        per_row.append(jnp.concatenate(heads, axis=-1))
    o = jnp.concatenate(per_row, axis=0)
    o = o * _silu(jnp.dot(hb, wxg_ref[...], preferred_element_type=F32))
    out_c = _dot(o, xow_ref[...])
    gate = _sigmoid(jnp.dot(hb, wg_ref[...], preferred_element_type=F32) + gb_ref[...])
    m = ma_ref[...].reshape(rows, D_MODEL) + mb_ref[...].reshape(rows, D_MODEL) + gate * out_c
    y_ref[...] = (x + _rms(_dot(m, wo_ref[...]), npost_ref[...])).reshape(bb, tt, D_MODEL)


def _attn_merge(x, ma, mb, mem_k, mem_v, layer, npw, wq, wxg, wg, gb, xow, wo, npost):
    b, t, _ = x.shape
    tt = _time_tile(t)
    bb = _rows_per_step(b, tt)
    nt = t // tt
    const = lambda i, j: (0, 0)
    wspec = lambda shape: pl.BlockSpec(shape, const, pipeline_mode=pl.Buffered(1))
    tile = pl.BlockSpec((bb, tt, D_MODEL), lambda i, j: (i, j, 0))
    mem = pl.BlockSpec((1, bb, N_MEM, XA_DIM), lambda i, j: (layer, i, 0, 0))
    return pl.pallas_call(
        functools.partial(_attn_merge_kernel, bb=bb, tt=tt),
        out_shape=jax.ShapeDtypeStruct((b, t, D_MODEL), F32),
        grid=(b // bb, nt),
        in_specs=[tile, tile, tile, mem, mem,
                  wspec((1, D_MODEL)),
                  wspec((D_MODEL, XA_DIM)),
                  wspec((D_MODEL, XA_DIM)),
                  wspec((D_MODEL, D_MODEL)),
                  wspec((1, D_MODEL)),
                  wspec((XA_DIM, D_MODEL)),
                  wspec((D_MODEL, D_MODEL)),
                  wspec((1, D_MODEL))],
        out_specs=tile,
        compiler_params=pltpu.CompilerParams(dimension_semantics=("arbitrary", "arbitrary"),
                                             vmem_limit_bytes=VMEM_LIMIT_BYTES),
        name="attn_merge",
    )(x, ma, mb, mem_k, mem_v, npw, wq, wxg, wg, gb, xow, wo, npost)


def _layer_weights(l, norm_pre_w, w_in, gate_b, conv_dw_w, conv_dw_b, conv_ln_w, conv_ln_b, conv_out_w,
                   ssd_conv_w, ssd_conv_b, ssd_dt_bias, ssd_a_log, ssd_d, ssd_norm_w, ssd_out_w,
                   xa_out_w, w_out, norm_post_w):
    offs = np.cumsum((0,) + IN_SIZES)
    seg = lambda i: w_in[l][:, offs[i]:offs[i + 1]].astype(BF16)
    row = lambda v: v.reshape(1, -1)
    pad_lanes = lambda v: jnp.pad(v, ((0, 0), (0, LANES - v.shape[-1])))
    gates_w = seg(8)
    gb = gate_b[l]
    return dict(
        npw=row(norm_pre_w[l]),
        w3=w_in[l][:, offs[0]:offs[3]].astype(BF16),
        wz=seg(3), wxbc=seg(4), wdt=pad_lanes(seg(5)), wq=seg(6), wxg=seg(7),
        wg=[gates_w[:, i * D_MODEL:(i + 1) * D_MODEL] for i in range(N_BRANCH)],
        gb=[row(gb[i * D_MODEL:(i + 1) * D_MODEL]) for i in range(N_BRANCH)],
        cw=conv_dw_w[l], cb=row(conv_dw_b[l]), lnw=row(conv_ln_w[l]), lnb=row(conv_ln_b[l]),
        cow=conv_out_w[l].astype(BF16),
        scw=ssd_conv_w[l], scb=row(ssd_conv_b[l]),
        dtb=pad_lanes(row(ssd_dt_bias[l])), alog=pad_lanes(row(ssd_a_log[l])),
        dfull=row(jnp.repeat(ssd_d[l], SSD_HEAD_DIM)), nw=row(ssd_norm_w[l]),
        wout=ssd_out_w[l].astype(BF16),
        xow=xa_out_w[l].astype(BF16), wo=w_out[l].astype(BF16), npost=row(norm_post_w[l]),
    )


def _layer(x, mem_k, mem_v, mem_layer, st_a, st_ssd, st_h, state_layer, w):
    ma, new_a = _conv_branch(x, st_a, w["npw"], w["w3"], w["wg"][0], w["gb"][0], w["cw"], w["cb"],
                             w["lnw"], w["lnb"], w["cow"])
    mb, new_ssd, new_h = _ssd_branch(x, st_ssd, st_h, state_layer, w["npw"], w["wz"], w["wxbc"], w["wdt"], w["wg"][1],
                                     w["gb"][1], w["scw"], w["scb"], w["dtb"], w["alog"], w["dfull"],
                                     w["nw"], w["wout"])
    y = _attn_merge(x, ma, mb, mem_k, mem_v, mem_layer, w["npw"], w["wq"], w["wxg"], w["wg"][2], w["gb"][2],
                    w["xow"], w["wo"], w["npost"])
    return y, new_a, new_ssd, new_h


def kernel(x_prompt, x_sample, mem_prompt, cache_mem_k, cache_mem_v, state_conv_a, state_conv_ssd, state_ssm,
           norm_pre_w, w_in, gate_b, conv_dw_w, conv_dw_b, conv_ln_w, conv_ln_b, conv_out_w,
           ssd_conv_w, ssd_conv_b, ssd_dt_bias, ssd_a_log, ssd_d, ssd_norm_w, ssd_out_w,
           mem_norm_w, xa_kv_w, xa_out_w, w_out, norm_post_w):
    bp = x_prompt.shape[0]
    bs = x_sample.shape[0]
    weights = [_layer_weights(l, norm_pre_w, w_in, gate_b, conv_dw_w, conv_dw_b, conv_ln_w, conv_ln_b,
                              conv_out_w, ssd_conv_w, ssd_conv_b, ssd_dt_bias, ssd_a_log, ssd_d, ssd_norm_w,
                              ssd_out_w, xa_out_w, w_out, norm_post_w) for l in range(DEPTH)]

    y_p = x_prompt
    mk_p, mv_p = _mem_kv(mem_prompt, mem_norm_w, xa_kv_w)
    zero_h = jnp.zeros((1, bp, SSD_HEADS, SSD_HEAD_DIM, SSD_STATE), F32)
    ca_p, cs_p, hs_p = [], [], []
    for l in range(DEPTH):
        y_p, na, ns, nh = _layer(y_p, mk_p, mv_p, l,
                                 jnp.zeros((bp, CONV_WIDTH - 1, CONV_DIM), F32),
                                 jnp.zeros((bp, SSD_CONV_WIDTH - 1, SSD_XBC), F32),
                                 zero_h, 0, weights[l])
        ca_p.append(na); cs_p.append(ns); hs_p.append(nh)

    y_s = x_sample
    mk_s = cache_mem_k.reshape(DEPTH, bs, N_MEM, XA_DIM)
    mv_s = cache_mem_v.reshape(DEPTH, bs, N_MEM, XA_DIM)
    ca_s, cs_s, hs_s = [], [], []
    for l in range(DEPTH):
        y_s, na, ns, nh = _layer(y_s, mk_s, mv_s, l, state_conv_a[l], state_conv_ssd[l], state_ssm, l, weights[l])
        ca_s.append(na); cs_s.append(ns); hs_s.append(nh)

    return (y_p, y_s,
            mk_p.reshape(DEPTH, bp, N_MEM, XA_HEADS, XA_HEAD_DIM),
            mv_p.reshape(DEPTH, bp, N_MEM, XA_HEADS, XA_HEAD_DIM),
            jnp.stack(ca_p), jnp.stack(cs_p), jnp.stack(hs_p),
            jnp.stack(ca_s), jnp.stack(cs_s), jnp.stack(hs_s))
```
